```python
import math
import jax
import jax.numpy as jnp
from jax import lax
import numpy as np

D_MODEL = 1024
BATCH = 4
SEQ = 8192
DEPTH = 4

CHUNK = 64
GDN_HEADS = 8
GDN_DK = 128
GDN_DV = 128
GDN_CONV = 4
GDN_QK = GDN_HEADS * GDN_DK
GDN_V = GDN_HEADS * GDN_DV
GDN_IN = 2 * GDN_QK + 2 * GDN_V + 2 * GDN_HEADS
GDN_DT_MIN = 1e-3
GDN_DT_MAX = 1e-1
S5_GROUP = 16
S5_GROUPS = D_MODEL // S5_GROUP
S5_STATE = 64
S5_DT_MIN = 1e-3
S5_DT_MAX = 1e-1
D_FF = 11 * D_MODEL // 4
N_EXPERTS = 8
TOP_K = 2
D_FF_EXPERT = D_FF // 2
N_EVEN = (DEPTH + 1) // 2
N_ODD = DEPTH // 2
ALPHA = (2 * DEPTH) ** 0.25
BETA_INIT = (8 * DEPTH) ** -0.25
LN_EPS = 1e-5
NORM_EPS = 1e-6

kernel_name = 'hybrid_gdn_s5_moe_deepnorm'


def layer_norm(x, g, b):
    xf = x.astype(jnp.float32)
    mu = jnp.mean(xf, -1, keepdims=True)
    var = jnp.mean(jnp.square(xf - mu), -1, keepdims=True)
    y = (xf - mu) * lax.rsqrt(var + LN_EPS) * g.astype(jnp.float32) + b.astype(jnp.float32)
    return y.astype(x.dtype)


def l2norm(t):
    return t * lax.rsqrt(jnp.sum(t * t, -1, keepdims=True) + NORM_EPS)


def causal_dwconv(x, w):
    k = w.shape[0]
    return lax.conv_general_dilated(x, w[:, None, :].astype(x.dtype), window_strides=(1,),
                                    padding=[(k - 1, 0)], dimension_numbers=('NWC', 'WIO', 'NWC'),
                                    feature_group_count=x.shape[-1])


def gated_delta_rule_chunked(q, k, v, g, beta):
    bsz, seq, nh, dk = q.shape
    dv = v.shape[-1]
    n = seq // CHUNK

    def to_chunks(t):
        t = t.reshape((bsz, n, CHUNK) + t.shape[2:])
        return jnp.moveaxis(t, 3, 1)

    q, k, v, g, beta = (to_chunks(t) for t in (q, k, v, g, beta))
    g = jnp.cumsum(g, axis=-1)
    idx = jnp.arange(CHUNK)
    causal = idx[:, None] >= idx[None, :]
    strict = idx[:, None] > idx[None, :]
    decay = jnp.exp(jnp.where(causal, g[..., :, None] - g[..., None, :], -jnp.inf))
    k_beta = k * beta[..., None]
    v_beta = v * beta[..., None]
    lmat = jnp.where(strict, jnp.einsum('bhncd,bhnsd->bhncs', k_beta, k) * decay, 0.0)
    eye = jnp.eye(CHUNK, dtype=jnp.float32)
    t_inv = lax.linalg.triangular_solve(eye + lmat, jnp.broadcast_to(eye, lmat.shape),
                                        left_side=True, lower=True, unit_diagonal=True)
    u = jnp.einsum('bhncs,bhnsd->bhncd', t_inv, v_beta)
    w = jnp.einsum('bhncs,bhnsd->bhncd', t_inv, k_beta * jnp.exp(g)[..., None])
    a_intra = jnp.einsum('bhncd,bhnsd->bhncs', q, k) * decay

    def step(state, inp):
        q_c, k_c, u_c, w_c, g_c, a_c = inp
        v_new = u_c - jnp.einsum('bhcd,bhde->bhce', w_c, state)
        o = (jnp.einsum('bhcd,bhde->bhce', q_c * jnp.exp(g_c)[..., None], state)
             + jnp.einsum('bhcs,bhse->bhce', a_c, v_new))
        g_last = g_c[..., -1]
        state = (state * jnp.exp(g_last)[..., None, None]
                 + jnp.einsum('bhcd,bhce->bhde', k_c * jnp.exp(g_last[..., None] - g_c)[..., None], v_new))
        return state, o

    xs = tuple(jnp.moveaxis(t, 2, 0) for t in (q, k, u, w, g, a_intra))
    s0 = jnp.zeros((bsz, nh, dk, dv), jnp.float32)
    _, o = lax.scan(step, s0, xs)
    o = jnp.moveaxis(jnp.moveaxis(o, 0, 2), 1, 3)
    return o.reshape(bsz, seq, nh, dv)


def gdn_mixer(x, w_in, conv_w, a_log, dt_bias, norm_g, w_out):
    bsz, seq, _ = x.shape
    f32 = jnp.float32
    proj = x @ w_in
    qkv, z, b, a = jnp.split(proj, [2 * GDN_QK + GDN_V, 2 * GDN_QK + 2 * GDN_V,
                                    2 * GDN_QK + 2 * GDN_V + GDN_HEADS], axis=-1)
    qkv = jax.nn.silu(causal_dwconv(qkv, conv_w)).astype(f32)
    q, k, v = jnp.split(qkv, [GDN_QK, 2 * GDN_QK], axis=-1)
    q = l2norm(q.reshape(bsz, seq, GDN_HEADS, GDN_DK)) * (GDN_DK ** -0.5)
    k = l2norm(k.reshape(bsz, seq, GDN_HEADS, GDN_DK))
    v = v.reshape(bsz, seq, GDN_HEADS, GDN_DV)
    beta = jax.nn.sigmoid(b.astype(f32))
    g = -jnp.exp(a_log.astype(f32)) * jax.nn.softplus(a.astype(f32) + dt_bias.astype(f32))
    o = gated_delta_rule_chunked(q, k, v, g, beta)
    o = o * lax.rsqrt(jnp.mean(o * o, -1, keepdims=True) + NORM_EPS) * norm_g.astype(f32)
    o = o * jax.nn.silu(z.reshape(bsz, seq, GDN_HEADS, GDN_DV).astype(f32))
    return o.reshape(bsz, seq, GDN_V).astype(x.dtype) @ w_out


def s5_mixer(x, w_in, lam_re, lam_im, log_dt, b_re, b_im, c_re, c_im, d_skip, w_glu):
    bsz, seq, _ = x.shape
    f32 = jnp.float32
    n = seq // CHUNK
    u = (x @ w_in).astype(f32)
    lam = lax.complex(lam_re.astype(f32), lam_im.astype(f32))
    dt = jnp.exp(log_dt.astype(f32))[:, None]
    a_bar = jnp.exp(lam * dt)
    b_bar = ((a_bar - 1.0) / lam)[..., None] * lax.complex(b_re.astype(f32), b_im.astype(f32))
    c = lax.complex(c_re.astype(f32), c_im.astype(f32))
    steps = jnp.arange(1, CHUNK + 1, dtype=f32)[:, None, None]
    a_pow = jnp.exp((lam * dt)[None] * steps)

    def binop(e1, e2):
        a1, b1 = e1
        a2, b2 = e2
        return a2 * a1, a2 * b1 + b2

    def step(h, u_c):
        bu = jnp.einsum('gpc,btgc->btgp', b_bar, u_c.astype(jnp.complex64))
        _, hs = lax.associative_scan(binop, (jnp.broadcast_to(a_bar, bu.shape), bu), axis=1)
        hs = hs + a_pow[None] * h[:, None]
        y = jnp.einsum('gcp,btgp->btgc', c, hs).real
        return hs[:, -1], y

    uc = jnp.moveaxis(u.reshape(bsz, n, CHUNK, S5_GROUPS, S5_GROUP), 1, 0)
    h0 = jnp.zeros((bsz, S5_GROUPS, S5_STATE), jnp.complex64)
    _, ys = lax.scan(step, h0, uc)
    y = jnp.moveaxis(ys, 0, 1).reshape(bsz, seq, D_MODEL) + d_skip.astype(f32) * u
    hid = jax.nn.gelu(y.astype(x.dtype), approximate=False)
    val, gate = jnp.split(hid @ w_glu, 2, axis=-1)
    return val * jax.nn.sigmoid(gate)


def swiglu(x, w1, w3, w2):
    return (jax.nn.silu(x @ w1) * (x @ w3)) @ w2


def moe_swiglu(x, w_router, b_router, w1, w3, w2):
    bsz, seq, d = x.shape
    xt = x.reshape(-1, d)
    logits = (xt @ w_router).astype(jnp.float32) + b_router.astype(jnp.float32)
    top_v, top_i = lax.top_k(logits, TOP_K)
    gates = jax.nn.softmax(top_v, axis=-1)
    dense_gate = jnp.sum(jax.nn.one_hot(top_i, N_EXPERTS, dtype=jnp.float32) * gates[..., None], axis=1)
    y = jnp.zeros_like(xt)
    for e in range(N_EXPERTS):
        y = y + dense_gate[:, e:e + 1].astype(x.dtype) * swiglu(xt, w1[e], w3[e], w2[e])
    return y.reshape(bsz, seq, d)


def setup_inputs(seed: int = 0) -> dict:
    key = jax.random.key(seed)
    ks = jax.random.split(key, 27)
    f32 = jnp.float32

    def nrm(i, shape, scale):
        return jax.random.normal(ks[i], shape, f32) * scale

    x = nrm(0, (BATCH, SEQ, D_MODEL), 1.0)
    gdn_w_in = nrm(1, (N_EVEN, D_MODEL, GDN_IN), D_MODEL ** -0.5)
    gdn_conv_w = nrm(2, (N_EVEN, GDN_CONV, 2 * GDN_QK + GDN_V), GDN_CONV ** -0.5)
    gdn_a_log = jnp.log(jax.random.uniform(ks[3], (N_EVEN, GDN_HEADS), f32, 1.0, 16.0))
    dt = jnp.exp(jax.random.uniform(ks[4], (N_EVEN, GDN_HEADS), f32, math.log(GDN_DT_MIN), math.log(GDN_DT_MAX)))
    gdn_dt_bias = dt + jnp.log(-jnp.expm1(-dt))
    gdn_norm_g = 1.0 + nrm(5, (N_EVEN, GDN_DV), 0.02)
    gdn_w_out = nrm(6, (N_EVEN, GDN_V, D_MODEL), GDN_V ** -0.5 * BETA_INIT)
    ffn_w1 = nrm(7, (N_EVEN, D_MODEL, D_FF), D_MODEL ** -0.5)
    ffn_w3 = nrm(8, (N_EVEN, D_MODEL, D_FF), D_MODEL ** -0.5)
    ffn_w2 = nrm(9, (N_EVEN, D_FF, D_MODEL), D_FF ** -0.5 * BETA_INIT)
    s5_w_in = nrm(10, (N_ODD, D_MODEL, D_MODEL), D_MODEL ** -0.5)
    s5_lam_re = -0.5 + nrm(11, (N_ODD, S5_GROUPS, S5_STATE), 0.01)
    s5_lam_im = math.pi * jnp.arange(S5_STATE, dtype=f32) + nrm(12, (N_ODD, S5_GROUPS, S5_STATE), 0.01)
    s5_log_dt = jax.random.uniform(ks[13], (N_ODD, S5_GROUPS), f32, math.log(S5_DT_MIN), math.log(S5_DT_MAX))
    s5_b_re = nrm(14, (N_ODD, S5_GROUPS, S5_STATE, S5_GROUP), (2 * S5_GROUP) ** -0.5)
    s5_b_im = nrm(15, (N_ODD, S5_GROUPS, S5_STATE, S5_GROUP), (2 * S5_GROUP) ** -0.5)
    s5_c_re = nrm(16, (N_ODD, S5_GROUPS, S5_GROUP, S5_STATE), S5_STATE ** -0.5)
    s5_c_im = nrm(17, (N_ODD, S5_GROUPS, S5_GROUP, S5_STATE), S5_STATE ** -0.5)
    s5_d = nrm(18, (N_ODD, D_MODEL), 1.0)
    s5_w_glu = nrm(19, (N_ODD, D_MODEL, 2 * D_MODEL), D_MODEL ** -0.5 * BETA_INIT)
    moe_w_router = nrm(20, (N_ODD, D_MODEL, N_EXPERTS), D_MODEL ** -0.5)
    moe_b_router = nrm(21, (N_ODD, N_EXPERTS), 0.01)
    moe_w1 = nrm(22, (N_ODD, N_EXPERTS, D_MODEL, D_FF_EXPERT), D_MODEL ** -0.5)
    moe_w3 = nrm(23, (N_ODD, N_EXPERTS, D_MODEL, D_FF_EXPERT), D_MODEL ** -0.5)
    moe_w2 = nrm(24, (N_ODD, N_EXPERTS, D_FF_EXPERT, D_MODEL), D_FF_EXPERT ** -0.5 * BETA_INIT)
    ln_g = 1.0 + nrm(25, (DEPTH, 2, D_MODEL), 0.02)
    ln_b = nrm(26, (DEPTH, 2, D_MODEL), 0.02)
    return {'x': x, 'gdn_w_in': gdn_w_in, 'gdn_conv_w': gdn_conv_w, 'gdn_a_log': gdn_a_log,
            'gdn_dt_bias': gdn_dt_bias, 'gdn_norm_g': gdn_norm_g, 'gdn_w_out': gdn_w_out,
            'ffn_w1': ffn_w1, 'ffn_w3': ffn_w3, 'ffn_w2': ffn_w2,
            's5_w_in': s5_w_in, 's5_lam_re': s5_lam_re, 's5_lam_im': s5_lam_im, 's5_log_dt': s5_log_dt,
            's5_b_re': s5_b_re, 's5_b_im': s5_b_im, 's5_c_re': s5_c_re, 's5_c_im': s5_c_im,
            's5_d': s5_d, 's5_w_glu': s5_w_glu,
            'moe_w_router': moe_w_router, 'moe_b_router': moe_b_router,
            'moe_w1': moe_w1, 'moe_w3': moe_w3, 'moe_w2': moe_w2,
            'ln_g': ln_g, 'ln_b': ln_b}


def reference(x, gdn_w_in, gdn_conv_w, gdn_a_log, gdn_dt_bias, gdn_norm_g, gdn_w_out,
              ffn_w1, ffn_w3, ffn_w2,
              s5_w_in, s5_lam_re, s5_lam_im, s5_log_dt, s5_b_re, s5_b_im, s5_c_re, s5_c_im,
              s5_d, s5_w_glu,
              moe_w_router, moe_b_router, moe_w1, moe_w3, moe_w2,
              ln_g, ln_b):
    for i in range(DEPTH):
        j = i // 2
        if i % 2 == 0:
            h = gdn_mixer(x, gdn_w_in[j], gdn_conv_w[j], gdn_a_log[j], gdn_dt_bias[j],
                          gdn_norm_g[j], gdn_w_out[j])
        else:
            h = s5_mixer(x, s5_w_in[j], s5_lam_re[j], s5_lam_im[j], s5_log_dt[j], s5_b_re[j],
                         s5_b_im[j], s5_c_re[j], s5_c_im[j], s5_d[j], s5_w_glu[j])
        x = layer_norm(ALPHA * x + h, ln_g[i, 0], ln_b[i, 0])
        if i % 2 == 0:
            f = swiglu(x, ffn_w1[j], ffn_w3[j], ffn_w2[j])
        else:
            f = moe_swiglu(x, moe_w_router[j], moe_b_router[j], moe_w1[j], moe_w3[j], moe_w2[j])
        x = layer_norm(ALPHA * x + f, ln_g[i, 1], ln_b[i, 1])
    return x
```

```python
import functools
import math

import jax
import jax.numpy as jnp
from jax import lax
from jax.experimental import pallas as pl
from jax.experimental.pallas import tpu as pltpu

F32 = jnp.float32
BF16 = jnp.bfloat16

CHUNK = 64
GDN_HEADS = 8
GDN_DK = 128
GDN_CONV = 4
S5_GROUP = 16
S5_STATE = 64
N_EXPERTS = 8
LN_EPS = 1e-5
NORM_EPS = 1e-6

VMEM_LIMIT_BYTES = 56 * 1024 * 1024
CONV_CARRY_ROWS = 8


def _params(*semantics):
    return pltpu.CompilerParams(dimension_semantics=semantics, vmem_limit_bytes=VMEM_LIMIT_BYTES)


def _const_spec(shape):
    nd = len(shape)
    return pl.BlockSpec(shape, lambda *_: (0,) * nd, pipeline_mode=pl.Buffered(1))


def _dot(a, b):
    return jnp.dot(a.astype(BF16), b.astype(BF16), preferred_element_type=F32)


def _dot_nt(a, b):
    return lax.dot_general(a.astype(BF16), b.astype(BF16), (((1,), (1,)), ((), ())),
                           preferred_element_type=F32)


def _dot_tn(a, b):
    return lax.dot_general(a.astype(BF16), b.astype(BF16), (((0,), (0,)), ((), ())),
                           preferred_element_type=F32)


def _dot_f32(a, b):
    return jnp.dot(a, b, preferred_element_type=F32, precision=lax.Precision.HIGHEST)


def _silu(x):
    return x * jax.nn.sigmoid(x)


def _gelu(x):
    return 0.5 * x * (1.0 + lax.erf(x * (2.0 ** -0.5)))


def _softplus(x):
    return jnp.maximum(x, 0.0) + jnp.log1p(jnp.exp(-jnp.abs(x)))


def _layer_norm(y, g, b):
    mu = jnp.mean(y, axis=-1, keepdims=True)
    yc = y - mu
    var = jnp.mean(yc * yc, axis=-1, keepdims=True)
    return yc * lax.rsqrt(var + LN_EPS) * g + b


def _gdn_in_kernel(x_ref, wqkv_ref, wz_ref, wb_ref, wa_ref, waT_ref, conv_ref, alog_ref, dtb_ref,
                   alogc_ref, dtbc_ref, tri_ref, triT_ref,
                   q_ref, k_ref, v_ref, z_ref, beta_ref, gc_ref, gcT_ref, ext_ref,
                   *, tiles_per_seq, tm, n_qk):
    i = pl.program_id(0)

    @pl.when(i % tiles_per_seq == 0)
    def _():
        ext_ref[0:CONV_CARRY_ROWS, :] = jnp.zeros((CONV_CARRY_ROWS, ext_ref.shape[1]), F32)

    xb = x_ref[...].astype(BF16)
    ext_ref[CONV_CARRY_ROWS:CONV_CARRY_ROWS + tm, :] = jnp.dot(xb, wqkv_ref[...], preferred_element_type=F32)
    z_ref[...] = jnp.dot(xb, wz_ref[...], preferred_element_type=F32)
    beta_ref[...] = jax.nn.sigmoid(jnp.dot(xb, wb_ref[...], preferred_element_type=F32))

    g_col = -jnp.exp(alog_ref[...]) * _softplus(jnp.dot(xb, wa_ref[...], preferred_element_type=F32) + dtb_ref[...])
    for c in range(tm // CHUNK):
        rows = slice(c * CHUNK, (c + 1) * CHUNK)
        gc_ref[rows, :] = _dot_f32(tri_ref[...], g_col[rows, :])
        a_row = lax.dot_general(waT_ref[...], xb[rows, :], (((1,), (1,)), ((), ())), preferred_element_type=F32)
        g_row = -jnp.exp(alogc_ref[...]) * _softplus(a_row + dtbc_ref[...])
        gcT_ref[c] = _dot_f32(g_row, triT_ref[...])

    n_blocks = ext_ref.shape[1] // GDN_DK
    base = CONV_CARRY_ROWS - (GDN_CONV - 1)
    for blk in range(n_blocks):
        cols = slice(blk * GDN_DK, (blk + 1) * GDN_DK)
        acc = conv_ref[0:1, cols] * ext_ref[base:base + tm, cols]
        for j in range(1, GDN_CONV):
            acc = acc + conv_ref[j:j + 1, cols] * ext_ref[base + j:base + j + tm, cols]
        y = _silu(acc)
        if blk < 2 * n_qk:
            y = y * lax.rsqrt(jnp.sum(y * y, axis=-1, keepdims=True) + NORM_EPS)
        if blk < n_qk:
            q_ref[:, cols] = y * (GDN_DK ** -0.5)
        elif blk < 2 * n_qk:
            k_ref[:, slice((blk - n_qk) * GDN_DK, (blk - n_qk + 1) * GDN_DK)] = y
        else:
            v_ref[:, slice((blk - 2 * n_qk) * GDN_DK, (blk - 2 * n_qk + 1) * GDN_DK)] = y
    ext_ref[0:CONV_CARRY_ROWS, :] = ext_ref[tm:tm + CONV_CARRY_ROWS, :]


def _gdn_in(x2, w_in, conv_w, a_log, dt_bias, seq, tm=256):
    t, d = x2.shape
    nh = GDN_HEADS
    qk = nh * GDN_DK
    dv = (w_in.shape[1] - 2 * qk - 2 * nh) // 2
    wqkv = w_in[:, :2 * qk + dv].astype(BF16)
    wz = w_in[:, 2 * qk + dv:2 * qk + 2 * dv].astype(BF16)
    wb = w_in[:, 2 * qk + 2 * dv:2 * qk + 2 * dv + nh].astype(BF16)
    wa = w_in[:, 2 * qk + 2 * dv + nh:].astype(BF16)
    idx = jnp.arange(CHUNK)
    tri = (idx[:, None] >= idx[None, :]).astype(F32)
    n_chunks = t // CHUNK
    kern = functools.partial(_gdn_in_kernel, tiles_per_seq=seq // tm, tm=tm, n_qk=nh)
    row = lambda i: (i, 0)
    return pl.pallas_call(
        kern,
        grid=(t // tm,),
        in_specs=[
            pl.BlockSpec((tm, d), row),
            _const_spec(wqkv.shape), _const_spec(wz.shape), _const_spec(wb.shape), _const_spec(wa.shape),
            _const_spec((nh, d)), _const_spec(conv_w.shape),
            _const_spec((1, nh)), _const_spec((1, nh)), _const_spec((nh, 1)), _const_spec((nh, 1)),
            _const_spec((CHUNK, CHUNK)), _const_spec((CHUNK, CHUNK)),
        ],
        out_specs=[
            pl.BlockSpec((tm, qk), row), pl.BlockSpec((tm, qk), row), pl.BlockSpec((tm, dv), row),
            pl.BlockSpec((tm, dv), row), pl.BlockSpec((tm, nh), row), pl.BlockSpec((tm, nh), row),
            pl.BlockSpec((tm // CHUNK, nh, CHUNK), lambda i: (i, 0, 0)),
        ],
        out_shape=[
            jax.ShapeDtypeStruct((t, qk), F32), jax.ShapeDtypeStruct((t, qk), F32),
            jax.ShapeDtypeStruct((t, dv), F32), jax.ShapeDtypeStruct((t, dv), F32),
            jax.ShapeDtypeStruct((t, nh), F32), jax.ShapeDtypeStruct((t, nh), F32),
            jax.ShapeDtypeStruct((n_chunks, nh, CHUNK), F32),
        ],
        scratch_shapes=[pltpu.VMEM((tm + CONV_CARRY_ROWS, 2 * qk + dv), F32)],
        compiler_params=_params("arbitrary"),
        name="gdn_in",
    )(x2, wqkv, wz, wb, wa, wa.T, conv_w.astype(F32),
      a_log.reshape(1, nh).astype(F32), dt_bias.reshape(1, nh).astype(F32),
      a_log.reshape(nh, 1).astype(F32), dt_bias.reshape(nh, 1).astype(F32), tri, tri.T)


def _unit_lower_inverse(lm):
    n = lm.shape[0]
    r = lax.broadcasted_iota(jnp.int32, (n, n), 0)
    c = lax.broadcasted_iota(jnp.int32, (n, n), 1)
    p = jnp.where(r == c, 1.0, 0.0) - lm
    m = lm
    k = 1
    while 2 * k <= n // 2:
        m = _dot(m, m)
        p = p + _dot(p, m)
        k *= 2
    return p


def _gdn_core_kernel(q_ref, k_ref, v_ref, z_ref, beta_ref, gc_ref, gcT_ref, ng_ref, o_ref, s_ref,
                     *, n_sub):
    nh = s_ref.shape[0]

    @pl.when(pl.program_id(1) == 0)
    def _():
        s_ref[...] = jnp.zeros(s_ref.shape, F32)

    r = lax.broadcasted_iota(jnp.int32, (CHUNK, CHUNK), 0)
    c = lax.broadcasted_iota(jnp.int32, (CHUNK, CHUNK), 1)
    causal = r >= c
    strict = r > c

    def chunk_body(ci, carry):
        rows = pl.ds(pl.multiple_of(ci * CHUNK, CHUNK), CHUNK)
        beta = beta_ref[rows, :]
        gc = gc_ref[rows, :]
        gct = gcT_ref[ci]
        for h in range(nh):
            cols = slice(h * GDN_DK, (h + 1) * GDN_DK)
            q = q_ref[rows, cols]
            k = k_ref[rows, cols]
            v = v_ref[rows, cols]
            g_col = gc[:, h:h + 1]
            g_row = gct[h:h + 1, :]
            b_col = beta[:, h:h + 1]
            decay = jnp.exp(jnp.where(causal, g_col - g_row, -jnp.inf))
            eg = jnp.exp(g_col)
            kb = k * b_col
            lmat = jnp.where(strict, _dot_nt(kb, k) * decay, 0.0)
            t_inv = _unit_lower_inverse(lmat)
            u = _dot(t_inv, v * b_col)
            w = _dot(t_inv, kb * eg)
            a_intra = _dot_nt(q, k) * decay
            s = s_ref[h]
            v_new = u - _dot(w, s)
            o = _dot(q * eg, s) + _dot(a_intra, v_new)
            g_last = g_col[CHUNK - 1:CHUNK, :]
            s_ref[h] = s * jnp.exp(g_last) + _dot_tn(k * jnp.exp(g_last - g_col), v_new)
            o = o * lax.rsqrt(jnp.mean(o * o, axis=-1, keepdims=True) + NORM_EPS) * ng_ref[...]
            o_ref[rows, cols] = (o * _silu(z_ref[rows, cols])).astype(o_ref.dtype)
        return carry

    lax.fori_loop(0, n_sub, chunk_body, 0)


def _gdn_core(q, k, v, z, beta, gc, gct, norm_g, bsz, seq, tc=512):
    t, qk = q.shape
    nh = GDN_HEADS
    dv = v.shape[1]
    tiles = seq // tc
    n_sub = tc // CHUNK
    row = lambda b, i: (b * tiles + i, 0)
    return pl.pallas_call(
        functools.partial(_gdn_core_kernel, n_sub=n_sub),
        grid=(bsz, tiles),
        in_specs=[
            pl.BlockSpec((tc, qk), row), pl.BlockSpec((tc, qk), row), pl.BlockSpec((tc, dv), row),
            pl.BlockSpec((tc, dv), row), pl.BlockSpec((tc, nh), row), pl.BlockSpec((tc, nh), row),
            pl.BlockSpec((n_sub, nh, CHUNK), lambda b, i: (b * tiles + i, 0, 0)),
            _const_spec((1, dv // nh)),
        ],
        out_specs=pl.BlockSpec((tc, dv), row),
        out_shape=jax.ShapeDtypeStruct((t, dv), BF16),
        scratch_shapes=[pltpu.VMEM((nh, GDN_DK, dv // nh), F32)],
        compiler_params=_params("arbitrary", "arbitrary"),
        name="gdn_core",
    )(q, k, v, z, beta, gc, gct, norm_g.reshape(1, -1).astype(F32))


def _proj_ln_kernel(a_ref, w_ref, x_ref, g_ref, b_ref, o_ref, *, alpha):
    h = jnp.dot(a_ref[...].astype(BF16), w_ref[...], preferred_element_type=F32)
    o_ref[...] = _layer_norm(alpha * x_ref[...] + h, g_ref[...], b_ref[...])


def _proj_ln(a, w, x2, ln_g, ln_b, alpha, tm=512):
    t, d = x2.shape
    row = lambda i: (i, 0)
    return pl.pallas_call(
        functools.partial(_proj_ln_kernel, alpha=alpha),
        grid=(t // tm,),
        in_specs=[pl.BlockSpec((tm, a.shape[1]), row), _const_spec(w.shape), pl.BlockSpec((tm, d), row),
                  _const_spec((1, d)), _const_spec((1, d))],
        out_specs=pl.BlockSpec((tm, d), row),
        out_shape=jax.ShapeDtypeStruct((t, d), F32),
        compiler_params=_params("arbitrary"),
        name="proj_ln",
    )(a, w.astype(BF16), x2, ln_g.reshape(1, d), ln_b.reshape(1, d))


def _ffn_ln_kernel(x_ref, w1_ref, w3_ref, w2_ref, g_ref, b_ref, o_ref, *, alpha):
    x = x_ref[...]
    xb = x.astype(BF16)
    hid = _silu(jnp.dot(xb, w1_ref[...], preferred_element_type=F32)) * jnp.dot(xb, w3_ref[...], preferred_element_type=F32)
    f = jnp.dot(hid.astype(BF16), w2_ref[...], preferred_element_type=F32)
    o_ref[...] = _layer_norm(alpha * x + f, g_ref[...], b_ref[...])


def _ffn_ln(x2, w1, w3, w2, ln_g, ln_b, alpha, tm=512):
    t, d = x2.shape
    row = lambda i: (i, 0)
    return pl.pallas_call(
        functools.partial(_ffn_ln_kernel, alpha=alpha),
        grid=(t // tm,),
        in_specs=[pl.BlockSpec((tm, d), row), _const_spec(w1.shape), _const_spec(w3.shape), _const_spec(w2.shape),
                  _const_spec((1, d)), _const_spec((1, d))],
        out_specs=pl.BlockSpec((tm, d), row),
        out_shape=jax.ShapeDtypeStruct((t, d), F32),
        compiler_params=_params("arbitrary"),
        name="ffn_ln",
    )(x2, w1.astype(BF16), w3.astype(BF16), w2.astype(BF16), ln_g.reshape(1, d), ln_b.reshape(1, d))


def _gdn_layer(x2, bsz, seq, w_in, conv_w, a_log, dt_bias, norm_g, w_out, ln_g, ln_b, alpha):
    q, k, v, z, beta, gc, gct = _gdn_in(x2, w_in, conv_w, a_log, dt_bias, seq)
    o = _gdn_core(q, k, v, z, beta, gc, gct, norm_g, bsz, seq)
    return _proj_ln(o, w_out, x2, ln_g, ln_b, alpha)


def _s5_taps_kernel(ce_ref, b_ref, o_ref):
    o_ref[0] = _dot_f32(ce_ref[0], b_ref[0])


def _s5_tables(lam_re, lam_im, log_dt, b_re, b_im, c_re, c_im, d_skip, n_per_seq):
    g, p = lam_re.shape
    gs = b_re.shape[-1]
    lr, li = lam_re.astype(F32), lam_im.astype(F32)
    dt = jnp.exp(log_dt.astype(F32))[:, None]
    zr, zi = lr * dt, li * dt

    def a_pow(steps):
        s = steps.astype(F32)[:, None, None]
        mag = jnp.exp(zr[None] * s)
        return mag * jnp.cos(zi[None] * s), mag * jnp.sin(zi[None] * s)

    ar, ai = a_pow(jnp.arange(CHUNK + 1))
    nr, ni = ar[1] - 1.0, ai[1]
    den = lr * lr + li * li
    qr, qi = (nr * lr + ni * li) / den, (ni * lr - nr * li) / den
    bbr = qr[..., None] * b_re - qi[..., None] * b_im
    bbi = qr[..., None] * b_im + qi[..., None] * b_re
    cr, ci = c_re.astype(F32), c_im.astype(F32)
    car = cr[:, None] * jnp.moveaxis(ar, 0, 1)[:, :, None, :] - ci[:, None] * jnp.moveaxis(ai, 0, 1)[:, :, None, :]
    cai = cr[:, None] * jnp.moveaxis(ai, 0, 1)[:, :, None, :] + ci[:, None] * jnp.moveaxis(ar, 0, 1)[:, :, None, :]
    ca = jnp.concatenate([car, -cai], axis=-1)
    ce = ca[:, :CHUNK].reshape(g, CHUNK * gs, 2 * p)
    bst = jnp.concatenate([bbr, bbi], axis=1)
    taps = pl.pallas_call(
        _s5_taps_kernel,
        grid=(g,),
        in_specs=[pl.BlockSpec((1, CHUNK * gs, 2 * p), lambda i: (i, 0, 0)),
                  pl.BlockSpec((1, 2 * p, gs), lambda i: (i, 0, 0))],
        out_specs=pl.BlockSpec((1, CHUNK * gs, gs), lambda i: (i, 0, 0)),
        out_shape=jax.ShapeDtypeStruct((g, CHUNK * gs, gs), F32),
        compiler_params=_params("arbitrary"),
        name="s5_taps",
    )(ce, bst).reshape(g, CHUNK, gs, gs)
    taps = taps.at[:, 0].add(d_skip.astype(F32).reshape(g, gs)[:, :, None] * jnp.eye(gs, dtype=F32))
    idx = jnp.arange(CHUNK)
    lag = idx[:, None] - idx[None, :]
    toe = jnp.where((lag >= 0)[None, :, :, None, None], taps[:, jnp.clip(lag, 0, CHUNK - 1)], 0.0)
    m_mat = jnp.transpose(toe, (0, 1, 3, 2, 4)).reshape(g, CHUNK * gs, CHUNK * gs).astype(BF16)
    arr, aii = jnp.moveaxis(ar[:CHUNK][::-1], 0, 1), jnp.moveaxis(ai[:CHUNK][::-1], 0, 1)
    wr = arr[:, :, :, None] * bbr[:, None] - aii[:, :, :, None] * bbi[:, None]
    wi = arr[:, :, :, None] * bbi[:, None] + aii[:, :, :, None] * bbr[:, None]
    w_mat = jnp.concatenate([jnp.transpose(wr, (0, 2, 1, 3)), jnp.transpose(wi, (0, 2, 1, 3))], axis=1)
    w_mat = w_mat.reshape(g, 2 * p, CHUNK * gs).astype(BF16)
    v_mat = ca[:, 1:].reshape(g, CHUNK * gs, 2 * p).astype(BF16)
    n_steps = max(1, (n_per_seq - 1).bit_length())
    apr, api = a_pow(CHUNK * 2 ** jnp.arange(n_steps))
    apr = jnp.moveaxis(apr, 0, 1)[..., None]
    api = jnp.moveaxis(api, 0, 1)[..., None]
    return m_mat, w_mat, v_mat, apr, api


def _s5_in_kernel(x_ref, wT_ref, o_ref):
    o_ref[0] = _dot_nt(wT_ref[...], x_ref[...]).astype(o_ref.dtype)


def _s5_scan_kernel(u_ref, m_ref, w_ref, v_ref, apr_ref, api_ref, o_ref, *, n_per_seq):
    rows, gs, cols = u_ref.shape
    p = apr_ref.shape[2]
    u = u_ref[...].reshape(rows * gs, cols)
    hin = jnp.dot(w_ref[0], u, preferred_element_type=F32)
    yr, yi = hin[:p], hin[p:]
    pos = lax.broadcasted_iota(jnp.int32, (p, cols), 1) % n_per_seq

    def shifted(a, d):
        return jnp.where(pos >= d, pltpu.roll(a, d, axis=1), 0.0)

    d, kk = 1, 0
    while d < n_per_seq:
        sr, si = shifted(yr, d), shifted(yi, d)
        ar, ai = apr_ref[0, kk], api_ref[0, kk]
        yr, yi = yr + ar * sr - ai * si, yi + ar * si + ai * sr
        d, kk = 2 * d, kk + 1
    h_prev = jnp.concatenate([shifted(yr, 1), shifted(yi, 1)], axis=0).astype(BF16)
    y = jnp.dot(m_ref[0], u, preferred_element_type=F32) + jnp.dot(v_ref[0], h_prev, preferred_element_type=F32)
    o_ref[0] = _gelu(y).reshape(rows, gs, cols).astype(o_ref.dtype)


def _s5_out_kernel(h_ref, w_ref, x_ref, g_ref, b_ref, o_ref, *, alpha):
    d = x_ref.shape[1]
    ht = h_ref[...].reshape(d, h_ref.shape[-1])
    vg = lax.dot_general(ht, w_ref[...], (((0,), (0,)), ((), ())), preferred_element_type=F32)
    o_ref[...] = _layer_norm(alpha * x_ref[...] + vg[:, :d] * jax.nn.sigmoid(vg[:, d:]), g_ref[...], b_ref[...])


def _s5_layer(x2, bsz, seq, w_in, lam_re, lam_im, log_dt, b_re, b_im, c_re, c_im, d_skip, w_glu, ln_g, ln_b, alpha):
    t, d = x2.shape
    n_per_seq = seq // CHUNK
    cols = t // CHUNK
    g, p = lam_re.shape
    gs = d // g
    m_mat, w_mat, v_mat, apr, api = _s5_tables(lam_re, lam_im, log_dt, b_re, b_im, c_re, c_im, d_skip, n_per_seq)
    xv = x2.reshape(cols, CHUNK * d)
    u_t = pl.pallas_call(
        _s5_in_kernel,
        grid=(CHUNK,),
        in_specs=[pl.BlockSpec((cols, d), lambda i: (0, i)), _const_spec((d, d))],
        out_specs=pl.BlockSpec((1, d, cols), lambda i: (i, 0, 0)),
        out_shape=jax.ShapeDtypeStruct((CHUNK, d, cols), BF16),
        compiler_params=_params("arbitrary"),
        name="s5_in",
    )(xv, w_in.T.astype(BF16))
    n_steps = apr.shape[1]
    hid = pl.pallas_call(
        functools.partial(_s5_scan_kernel, n_per_seq=n_per_seq),
        grid=(g,),
        in_specs=[pl.BlockSpec((CHUNK, gs, cols), lambda i: (0, i, 0)),
                  pl.BlockSpec((1, CHUNK * gs, CHUNK * gs), lambda i: (i, 0, 0)),
                  pl.BlockSpec((1, 2 * p, CHUNK * gs), lambda i: (i, 0, 0)),
                  pl.BlockSpec((1, CHUNK * gs, 2 * p), lambda i: (i, 0, 0)),
                  pl.BlockSpec((1, n_steps, p, 1), lambda i: (i, 0, 0, 0)),
                  pl.BlockSpec((1, n_steps, p, 1), lambda i: (i, 0, 0, 0))],
        out_specs=pl.BlockSpec((1, CHUNK, gs, cols), lambda i: (i, 0, 0, 0)),
        out_shape=jax.ShapeDtypeStruct((g, CHUNK, gs, cols), BF16),
        compiler_params=_params("arbitrary"),
        name="s5_scan",
    )(u_t, m_mat, w_mat, v_mat, apr, api)
    out = pl.pallas_call(
        functools.partial(_s5_out_kernel, alpha=alpha),
        grid=(CHUNK,),
        in_specs=[pl.BlockSpec((g, 1, gs, cols), lambda i: (0, i, 0, 0)), _const_spec(w_glu.shape),
                  pl.BlockSpec((cols, d), lambda i: (0, i)), _const_spec((1, d)), _const_spec((1, d))],
        out_specs=pl.BlockSpec((cols, d), lambda i: (0, i)),
        out_shape=jax.ShapeDtypeStruct((cols, CHUNK * d), F32),
        compiler_params=_params("arbitrary"),
        name="s5_out",
    )(hid, w_glu.astype(BF16), xv, ln_g.reshape(1, d), ln_b.reshape(1, d))
    return out.reshape(t, d)


def _top2_gates(logits):
    n_e = logits.shape[-1]
    lane = lax.broadcasted_iota(jnp.int32, logits.shape, 1)
    m1 = jnp.max(logits, axis=-1, keepdims=True)
    i1 = jnp.min(jnp.where(logits == m1, lane, n_e), axis=-1, keepdims=True)
    rest = jnp.where(lane == i1, -jnp.inf, logits)
    m2 = jnp.max(rest, axis=-1, keepdims=True)
    i2 = jnp.min(jnp.where(rest == m2, lane, n_e), axis=-1, keepdims=True)
    e2 = jnp.exp(m2 - m1)
    den = 1.0 + e2
    return jnp.where(lane == i1, 1.0 / den, 0.0) + jnp.where(lane == i2, e2 / den, 0.0)


def _moe_dense_kernel(x_ref, wr_ref, br_ref, w1_ref, w3_ref, w2_ref, g_ref, b_ref, o_ref, gate_ref, acc_ref,
                      *, alpha):
    e = pl.program_id(1)
    x = x_ref[...]

    @pl.when(e == 0)
    def _():
        gate_ref[...] = _top2_gates(_dot_f32(x, wr_ref[...]) + br_ref[...])
        acc_ref[...] = jnp.zeros(acc_ref.shape, F32)

    xb = x.astype(BF16)
    hid = _silu(jnp.dot(xb, w1_ref[0], preferred_element_type=F32)) * jnp.dot(xb, w3_ref[0], preferred_element_type=F32)
    f = jnp.dot(hid.astype(BF16), w2_ref[0], preferred_element_type=F32)
    lane = lax.broadcasted_iota(jnp.int32, gate_ref.shape, 1)
    gate = jnp.sum(jnp.where(lane == e, gate_ref[...], 0.0), axis=-1, keepdims=True)
    acc_ref[...] += gate * f

    @pl.when(e == pl.num_programs(1) - 1)
    def _():
        o_ref[...] = _layer_norm(alpha * x + acc_ref[...], g_ref[...], b_ref[...])


def _moe_layer(x2, w_router, b_router, w1, w3, w2, ln_g, ln_b, alpha, tm=1024):
    t, d = x2.shape
    n_e, _, dff = w1.shape
    row = lambda i, e: (i, 0)
    return pl.pallas_call(
        functools.partial(_moe_dense_kernel, alpha=alpha),
        grid=(t // tm, n_e),
        in_specs=[pl.BlockSpec((tm, d), row), _const_spec((d, n_e)), _const_spec((1, n_e)),
                  pl.BlockSpec((1, d, dff), lambda i, e: (e, 0, 0)), pl.BlockSpec((1, d, dff), lambda i, e: (e, 0, 0)),
                  pl.BlockSpec((1, dff, d), lambda i, e: (e, 0, 0)), _const_spec((1, d)), _const_spec((1, d))],
        out_specs=pl.BlockSpec((tm, d), row),
        out_shape=jax.ShapeDtypeStruct((t, d), F32),
        scratch_shapes=[pltpu.VMEM((tm, n_e), F32), pltpu.VMEM((tm, d), F32)],
        compiler_params=_params("arbitrary", "arbitrary"),
        name="moe_dense",
    )(x2, w_router.astype(F32), b_router.reshape(1, n_e).astype(F32), w1.astype(BF16), w3.astype(BF16),
      w2.astype(BF16), ln_g.reshape(1, d), ln_b.reshape(1, d))


def kernel(x, gdn_w_in, gdn_conv_w, gdn_a_log, gdn_dt_bias, gdn_norm_g, gdn_w_out, ffn_w1, ffn_w3, ffn_w2,
           s5_w_in, s5_lam_re, s5_lam_im, s5_log_dt, s5_b_re, s5_b_im, s5_c_re, s5_c_im, s5_d, s5_w_glu,
           moe_w_router, moe_b_router, moe_w1, moe_w3, moe_w2, ln_g, ln_b):
    bsz, seq, d = x.shape
    depth = ln_g.shape[0]
    alpha = (2 * depth) ** 0.25
    x2 = x.reshape(bsz * seq, d)
    for i in range(depth):
        j = i // 2
        if i % 2 == 0:
            x2 = _gdn_layer(x2, bsz, seq, gdn_w_in[j], gdn_conv_w[j], gdn_a_log[j], gdn_dt_bias[j], gdn_norm_g[j],
                            gdn_w_out[j], ln_g[i, 0], ln_b[i, 0], alpha)
            x2 = _ffn_ln(x2, ffn_w1[j], ffn_w3[j], ffn_w2[j], ln_g[i, 1], ln_b[i, 1], alpha)
        else:
            x2 = _s5_layer(x2, bsz, seq, s5_w_in[j], s5_lam_re[j], s5_lam_im[j], s5_log_dt[j], s5_b_re[j], s5_b_im[j],
                           s5_c_re[j], s5_c_im[j], s5_d[j], s5_w_glu[j], ln_g[i, 0], ln_b[i, 0], alpha)
            x2 = _moe_layer(x2, moe_w_router[j], moe_b_router[j], moe_w1[j], moe_w3[j], moe_w2[j],
                            ln_g[i, 1], ln_b[i, 1], alpha)
    return x2.reshape(bsz, seq, d)
```

```python
import functools
import math

import jax
import jax.numpy as jnp
from jax import lax
from jax.experimental import pallas as pl
from jax.experimental.pallas import tpu as pltpu

F32 = jnp.float32
BF16 = jnp.bfloat16

CHUNK = 64
GDN_HEADS = 8
GDN_DK = 128
GDN_CONV = 4
S5_GROUP = 16
S5_STATE = 64
N_EXPERTS = 8
LN_EPS = 1e-5
NORM_EPS = 1e-6

VMEM_LIMIT_BYTES = 56 * 1024 * 1024
CONV_CARRY_ROWS = 8


def _params(*semantics):
    return pltpu.CompilerParams(dimension_semantics=semantics, vmem_limit_bytes=VMEM_LIMIT_BYTES)


def _const_spec(shape):
    nd = len(shape)
    return pl.BlockSpec(shape, lambda *_: (0,) * nd, pipeline_mode=pl.Buffered(1))


def _dot(a, b):
    return jnp.dot(a.astype(BF16), b.astype(BF16), preferred_element_type=F32)


def _dot_nt(a, b):
    return lax.dot_general(a.astype(BF16), b.astype(BF16), (((1,), (1,)), ((), ())),
                           preferred_element_type=F32)


def _dot_tn(a, b):
    return lax.dot_general(a.astype(BF16), b.astype(BF16), (((0,), (0,)), ((), ())),
                           preferred_element_type=F32)


def _dot_f32(a, b):
    return jnp.dot(a, b, preferred_element_type=F32, precision=lax.Precision.HIGHEST)


def _silu(x):
    return x * jax.nn.sigmoid(x)


def _gelu(x):
    return 0.5 * x * (1.0 + lax.erf(x * (2.0 ** -0.5)))


def _softplus(x):
    return jnp.maximum(x, 0.0) + jnp.log1p(jnp.exp(-jnp.abs(x)))


def _layer_norm(y, g, b):
    mu = jnp.mean(y, axis=-1, keepdims=True)
    yc = y - mu
    var = jnp.mean(yc * yc, axis=-1, keepdims=True)
    return yc * lax.rsqrt(var + LN_EPS) * g + b


def _gdn_in_kernel(x_ref, wqkv_ref, wz_ref, wb_ref, wa_ref, waT_ref, conv_ref, alog_ref, dtb_ref,
                   alogc_ref, dtbc_ref, tri_ref, triT_ref,
                   q_ref, k_ref, v_ref, z_ref, beta_ref, gc_ref, gcT_ref, ext_ref,
                   *, tiles_per_seq, tm, n_qk):
    i = pl.program_id(0)

    @pl.when(i % tiles_per_seq == 0)
    def _():
        ext_ref[0:CONV_CARRY_ROWS, :] = jnp.zeros((CONV_CARRY_ROWS, ext_ref.shape[1]), F32)

    xb = x_ref[...].astype(BF16)
    ext_ref[CONV_CARRY_ROWS:CONV_CARRY_ROWS + tm, :] = jnp.dot(xb, wqkv_ref[...], preferred_element_type=F32)
    z_ref[...] = jnp.dot(xb, wz_ref[...], preferred_element_type=F32)
    beta_ref[...] = jax.nn.sigmoid(jnp.dot(xb, wb_ref[...], preferred_element_type=F32))

    g_col = -jnp.exp(alog_ref[...]) * _softplus(jnp.dot(xb, wa_ref[...], preferred_element_type=F32) + dtb_ref[...])
    for c in range(tm // CHUNK):
        rows = slice(c * CHUNK, (c + 1) * CHUNK)
        gc_ref[rows, :] = _dot_f32(tri_ref[...], g_col[rows, :])
        a_row = lax.dot_general(waT_ref[...], xb[rows, :], (((1,), (1,)), ((), ())), preferred_element_type=F32)
        g_row = -jnp.exp(alogc_ref[...]) * _softplus(a_row + dtbc_ref[...])
        gcT_ref[c] = _dot_f32(g_row, triT_ref[...])

    n_blocks = ext_ref.shape[1] // GDN_DK
    base = CONV_CARRY_ROWS - (GDN_CONV - 1)
    for blk in range(n_blocks):
        cols = slice(blk * GDN_DK, (blk + 1) * GDN_DK)
        acc = conv_ref[0:1, cols] * ext_ref[base:base + tm, cols]
        for j in range(1, GDN_CONV):
            acc = acc + conv_ref[j:j + 1, cols] * ext_ref[base + j:base + j + tm, cols]
        y = _silu(acc)
        if blk < 2 * n_qk:
            y = y * lax.rsqrt(jnp.sum(y * y, axis=-1, keepdims=True) + NORM_EPS)
        if blk < n_qk:
            q_ref[:, cols] = y * (GDN_DK ** -0.5)
        elif blk < 2 * n_qk:
            k_ref[:, slice((blk - n_qk) * GDN_DK, (blk - n_qk + 1) * GDN_DK)] = y
        else:
            v_ref[:, slice((blk - 2 * n_qk) * GDN_DK, (blk - 2 * n_qk + 1) * GDN_DK)] = y
    ext_ref[0:CONV_CARRY_ROWS, :] = ext_ref[tm:tm + CONV_CARRY_ROWS, :]


def _gdn_in(x2, w_in, conv_w, a_log, dt_bias, seq, tm=256):
    t, d = x2.shape
    nh = GDN_HEADS
    qk = nh * GDN_DK
    dv = (w_in.shape[1] - 2 * qk - 2 * nh) // 2
    wqkv = w_in[:, :2 * qk + dv].astype(BF16)
    wz = w_in[:, 2 * qk + dv:2 * qk + 2 * dv].astype(BF16)
    wb = w_in[:, 2 * qk + 2 * dv:2 * qk + 2 * dv + nh].astype(BF16)
    wa = w_in[:, 2 * qk + 2 * dv + nh:].astype(BF16)
    idx = jnp.arange(CHUNK)
    tri = (idx[:, None] >= idx[None, :]).astype(F32)
    n_chunks = t // CHUNK
    kern = functools.partial(_gdn_in_kernel, tiles_per_seq=seq // tm, tm=tm, n_qk=nh)
    row = lambda i: (i, 0)
    return pl.pallas_call(
        kern,
        grid=(t // tm,),
        in_specs=[
            pl.BlockSpec((tm, d), row),
            _const_spec(wqkv.shape), _const_spec(wz.shape), _const_spec(wb.shape), _const_spec(wa.shape),
            _const_spec((nh, d)), _const_spec(conv_w.shape),
            _const_spec((1, nh)), _const_spec((1, nh)), _const_spec((nh, 1)), _const_spec((nh, 1)),
            _const_spec((CHUNK, CHUNK)), _const_spec((CHUNK, CHUNK)),
        ],
        out_specs=[
            pl.BlockSpec((tm, qk), row), pl.BlockSpec((tm, qk), row), pl.BlockSpec((tm, dv), row),
            pl.BlockSpec((tm, dv), row), pl.BlockSpec((tm, nh), row), pl.BlockSpec((tm, nh), row),
            pl.BlockSpec((tm // CHUNK, nh, CHUNK), lambda i: (i, 0, 0)),
        ],
        out_shape=[
            jax.ShapeDtypeStruct((t, qk), F32), jax.ShapeDtypeStruct((t, qk), F32),
            jax.ShapeDtypeStruct((t, dv), F32), jax.ShapeDtypeStruct((t, dv), F32),
            jax.ShapeDtypeStruct((t, nh), F32), jax.ShapeDtypeStruct((t, nh), F32),
            jax.ShapeDtypeStruct((n_chunks, nh, CHUNK), F32),
        ],
        scratch_shapes=[pltpu.VMEM((tm + CONV_CARRY_ROWS, 2 * qk + dv), F32)],
        compiler_params=_params("arbitrary"),
        name="gdn_in",
    )(x2, wqkv, wz, wb, wa, wa.T, conv_w.astype(F32),
      a_log.reshape(1, nh).astype(F32), dt_bias.reshape(1, nh).astype(F32),
      a_log.reshape(nh, 1).astype(F32), dt_bias.reshape(nh, 1).astype(F32), tri, tri.T)


def _gdn_core_kernel(q_ref, k_ref, v_ref, z_ref, beta_ref, gc_ref, gcT_ref, ng_ref, o_ref,
                     s_ref, u_ref, wq_ref, kd_ref, a_ref, *, n_sub):
    nh = s_ref.shape[0]
    heads = range(nh)
    cols = [slice(h * GDN_DK, (h + 1) * GDN_DK) for h in heads]

    @pl.when(pl.program_id(1) == 0)
    def _():
        s_ref[...] = jnp.zeros(s_ref.shape, F32)

    r = lax.broadcasted_iota(jnp.int32, (CHUNK, CHUNK), 0)
    c = lax.broadcasted_iota(jnp.int32, (CHUNK, CHUNK), 1)
    causal = r >= c
    strict = r > c
    eye = jnp.where(r == c, 1.0, 0.0)

    def prep(ci, carry):
        rows = pl.ds(pl.multiple_of(ci * CHUNK, CHUNK), CHUNK)
        beta = beta_ref[rows, :]
        gc = gc_ref[rows, :]
        gct = gcT_ref[ci]
        q = [q_ref[rows, cols[h]] for h in heads]
        k = [k_ref[rows, cols[h]] for h in heads]
        g_col = [gc[:, h:h + 1] for h in heads]
        b_col = [beta[:, h:h + 1] for h in heads]
        decay = [jnp.exp(jnp.where(causal, g_col[h] - gct[h:h + 1, :], -jnp.inf)) for h in heads]
        eg = [jnp.exp(g_col[h]) for h in heads]
        kb = [k[h] * b_col[h] for h in heads]
        m = [jnp.where(strict, _dot_nt(kb[h], k[h]) * decay[h], 0.0) for h in heads]
        qk = [_dot_nt(q[h], k[h]) for h in heads]
        p = [eye - m[h] for h in heads]
        step = 1
        while 2 * step <= CHUNK // 2:
            m = [_dot(m[h], m[h]) for h in heads]
            p = [p[h] + _dot(p[h], m[h]) for h in heads]
            step *= 2
        rhs = [jnp.concatenate([v_ref[rows, cols[h]] * b_col[h], kb[h] * eg[h]], axis=1) for h in heads]
        uw = [_dot(p[h], rhs[h]) for h in heads]
        g_last = gc[CHUNK - 1:CHUNK, :]
        for h in heads:
            u_ref[ci, h] = uw[h][:, :GDN_DK]
            wq_ref[ci, h, 0:CHUNK, :] = uw[h][:, GDN_DK:].astype(BF16)
            wq_ref[ci, h, CHUNK:2 * CHUNK, :] = (q[h] * eg[h]).astype(BF16)
            kd_ref[ci, h] = (k[h] * jnp.exp(g_last[:, h:h + 1] - g_col[h])).astype(BF16)
            a_ref[ci, h] = (qk[h] * decay[h]).astype(BF16)
        return carry

    lax.fori_loop(0, n_sub, prep, 0)

    def scan(ci, carry):
        rows = pl.ds(pl.multiple_of(ci * CHUNK, CHUNK), CHUNK)
        e_last = jnp.exp(gc_ref[pl.ds(ci * CHUNK + CHUNK - 1, 1), :])
        s = [s_ref[h] for h in heads]
        ws = [jnp.dot(wq_ref[ci, h], s[h].astype(BF16), preferred_element_type=F32) for h in heads]
        v_new = [(u_ref[ci, h] - ws[h][:CHUNK]).astype(BF16) for h in heads]
        o = [ws[h][CHUNK:] + jnp.dot(a_ref[ci, h], v_new[h], preferred_element_type=F32) for h in heads]
        upd = [lax.dot_general(kd_ref[ci, h], v_new[h], (((0,), (0,)), ((), ())), preferred_element_type=F32)
               for h in heads]
        for h in heads:
            s_ref[h] = s[h] * e_last[:, h:h + 1] + upd[h]
            on = o[h] * lax.rsqrt(jnp.mean(o[h] * o[h], axis=-1, keepdims=True) + NORM_EPS) * ng_ref[...]
            o_ref[rows, cols[h]] = (on * _silu(z_ref[rows, cols[h]])).astype(o_ref.dtype)
        return carry

    lax.fori_loop(0, n_sub, scan, 0)


def _gdn_core(q, k, v, z, beta, gc, gct, norm_g, bsz, seq, tc=512):
    t, qk = q.shape
    nh = GDN_HEADS
    dv = v.shape[1]
    tiles = seq // tc
    n_sub = tc // CHUNK
    row = lambda b, i: (b * tiles + i, 0)
    return pl.pallas_call(
        functools.partial(_gdn_core_kernel, n_sub=n_sub),
        grid=(bsz, tiles),
        in_specs=[
            pl.BlockSpec((tc, qk), row), pl.BlockSpec((tc, qk), row), pl.BlockSpec((tc, dv), row),
            pl.BlockSpec((tc, dv), row), pl.BlockSpec((tc, nh), row), pl.BlockSpec((tc, nh), row),
            pl.BlockSpec((n_sub, nh, CHUNK), lambda b, i: (b * tiles + i, 0, 0)),
            _const_spec((1, dv // nh)),
        ],
        out_specs=pl.BlockSpec((tc, dv), row),
        out_shape=jax.ShapeDtypeStruct((t, dv), BF16),
        scratch_shapes=[pltpu.VMEM((nh, GDN_DK, dv // nh), F32),
                        pltpu.VMEM((n_sub, nh, CHUNK, dv // nh), F32),
                        pltpu.VMEM((n_sub, nh, 2 * CHUNK, GDN_DK), BF16),
                        pltpu.VMEM((n_sub, nh, CHUNK, GDN_DK), BF16),
                        pltpu.VMEM((n_sub, nh, CHUNK, CHUNK), BF16)],
        compiler_params=_params("arbitrary", "arbitrary"),
        name="gdn_core",
    )(q, k, v, z, beta, gc, gct, norm_g.reshape(1, -1).astype(F32))


def _proj_ln_kernel(a_ref, w_ref, x_ref, g_ref, b_ref, o_ref, *, alpha):
    h = jnp.dot(a_ref[...].astype(BF16), w_ref[...], preferred_element_type=F32)
    o_ref[...] = _layer_norm(alpha * x_ref[...] + h, g_ref[...], b_ref[...])


def _proj_ln(a, w, x2, ln_g, ln_b, alpha, tm=512):
    t, d = x2.shape
    row = lambda i: (i, 0)
    return pl.pallas_call(
        functools.partial(_proj_ln_kernel, alpha=alpha),
        grid=(t // tm,),
        in_specs=[pl.BlockSpec((tm, a.shape[1]), row), _const_spec(w.shape), pl.BlockSpec((tm, d), row),
                  _const_spec((1, d)), _const_spec((1, d))],
        out_specs=pl.BlockSpec((tm, d), row),
        out_shape=jax.ShapeDtypeStruct((t, d), F32),
        compiler_params=_params("arbitrary"),
        name="proj_ln",
    )(a, w.astype(BF16), x2, ln_g.reshape(1, d), ln_b.reshape(1, d))


def _ffn_ln_kernel(x_ref, w1_ref, w3_ref, w2_ref, g_ref, b_ref, o_ref, *, alpha):
    x = x_ref[...]
    xb = x.astype(BF16)
    hid = _silu(jnp.dot(xb, w1_ref[...], preferred_element_type=F32)) * jnp.dot(xb, w3_ref[...], preferred_element_type=F32)
    f = jnp.dot(hid.astype(BF16), w2_ref[...], preferred_element_type=F32)
    o_ref[...] = _layer_norm(alpha * x + f, g_ref[...], b_ref[...])


def _ffn_ln(x2, w1, w3, w2, ln_g, ln_b, alpha, tm=512):
    t, d = x2.shape
    row = lambda i: (i, 0)
    return pl.pallas_call(
        functools.partial(_ffn_ln_kernel, alpha=alpha),
        grid=(t // tm,),
        in_specs=[pl.BlockSpec((tm, d), row), _const_spec(w1.shape), _const_spec(w3.shape), _const_spec(w2.shape),
                  _const_spec((1, d)), _const_spec((1, d))],
        out_specs=pl.BlockSpec((tm, d), row),
        out_shape=jax.ShapeDtypeStruct((t, d), F32),
        compiler_params=_params("arbitrary"),
        name="ffn_ln",
    )(x2, w1.astype(BF16), w3.astype(BF16), w2.astype(BF16), ln_g.reshape(1, d), ln_b.reshape(1, d))


def _gdn_layer(x2, bsz, seq, w_in, conv_w, a_log, dt_bias, norm_g, w_out, ln_g, ln_b, alpha):
    q, k, v, z, beta, gc, gct = _gdn_in(x2, w_in, conv_w, a_log, dt_bias, seq)
    o = _gdn_core(q, k, v, z, beta, gc, gct, norm_g, bsz, seq)
    return _proj_ln(o, w_out, x2, ln_g, ln_b, alpha)


def _s5_taps_kernel(ce_ref, b_ref, o_ref):
    o_ref[0] = _dot_f32(ce_ref[0], b_ref[0])


def _s5_tables(lam_re, lam_im, log_dt, b_re, b_im, c_re, c_im, d_skip, n_per_seq):
    g, p = lam_re.shape
    gs = b_re.shape[-1]
    lr, li = lam_re.astype(F32), lam_im.astype(F32)
    dt = jnp.exp(log_dt.astype(F32))[:, None]
    zr, zi = lr * dt, li * dt

    def a_pow(steps):
        s = steps.astype(F32)[:, None, None]
        mag = jnp.exp(zr[None] * s)
        return mag * jnp.cos(zi[None] * s), mag * jnp.sin(zi[None] * s)

    ar, ai = a_pow(jnp.arange(CHUNK + 1))
    nr, ni = ar[1] - 1.0, ai[1]
    den = lr * lr + li * li
    qr, qi = (nr * lr + ni * li) / den, (ni * lr - nr * li) / den
    bbr = qr[..., None] * b_re - qi[..., None] * b_im
    bbi = qr[..., None] * b_im + qi[..., None] * b_re
    cr, ci = c_re.astype(F32), c_im.astype(F32)
    car = cr[:, None] * jnp.moveaxis(ar, 0, 1)[:, :, None, :] - ci[:, None] * jnp.moveaxis(ai, 0, 1)[:, :, None, :]
    cai = cr[:, None] * jnp.moveaxis(ai, 0, 1)[:, :, None, :] + ci[:, None] * jnp.moveaxis(ar, 0, 1)[:, :, None, :]
    ca = jnp.concatenate([car, -cai], axis=-1)
    ce = ca[:, :CHUNK].reshape(g, CHUNK * gs, 2 * p)
    bst = jnp.concatenate([bbr, bbi], axis=1)
    taps = pl.pallas_call(
        _s5_taps_kernel,
        grid=(g,),
        in_specs=[pl.BlockSpec((1, CHUNK * gs, 2 * p), lambda i: (i, 0, 0)),
                  pl.BlockSpec((1, 2 * p, gs), lambda i: (i, 0, 0))],
        out_specs=pl.BlockSpec((1, CHUNK * gs, gs), lambda i: (i, 0, 0)),
        out_shape=jax.ShapeDtypeStruct((g, CHUNK * gs, gs), F32),
        compiler_params=_params("arbitrary"),
        name="s5_taps",
    )(ce, bst).reshape(g, CHUNK, gs, gs)
    taps = taps.at[:, 0].add(d_skip.astype(F32).reshape(g, gs)[:, :, None] * jnp.eye(gs, dtype=F32))
    idx = jnp.arange(CHUNK)
    lag = idx[:, None] - idx[None, :]
    toe = jnp.where((lag >= 0)[None, :, :, None, None], taps[:, jnp.clip(lag, 0, CHUNK - 1)], 0.0)
    m_mat = jnp.transpose(toe, (0, 1, 3, 2, 4)).reshape(g, CHUNK * gs, CHUNK * gs).astype(BF16)
    arr, aii = jnp.moveaxis(ar[:CHUNK][::-1], 0, 1), jnp.moveaxis(ai[:CHUNK][::-1], 0, 1)
    wr = arr[:, :, :, None] * bbr[:, None] - aii[:, :, :, None] * bbi[:, None]
    wi = arr[:, :, :, None] * bbi[:, None] + aii[:, :, :, None] * bbr[:, None]
    w_mat = jnp.concatenate([jnp.transpose(wr, (0, 2, 1, 3)), jnp.transpose(wi, (0, 2, 1, 3))], axis=1)
    w_mat = w_mat.reshape(g, 2 * p, CHUNK * gs).astype(BF16)
    v_mat = ca[:, 1:].reshape(g, CHUNK * gs, 2 * p).astype(BF16)
    n_steps = max(1, (n_per_seq - 1).bit_length())
    apr, api = a_pow(CHUNK * 2 ** jnp.arange(n_steps))
    apr = jnp.moveaxis(apr, 0, 1)[..., None]
    api = jnp.moveaxis(api, 0, 1)[..., None]
    return m_mat, w_mat, v_mat, apr, api


def _s5_in_kernel(x_ref, wT_ref, o_ref):
    o_ref[0] = _dot_nt(wT_ref[...], x_ref[...]).astype(o_ref.dtype)


def _s5_scan_kernel(u_ref, m_ref, w_ref, v_ref, apr_ref, api_ref, o_ref, *, n_per_seq):
    rows, gs, cols = u_ref.shape
    p = apr_ref.shape[2]
    u = u_ref[...].reshape(rows * gs, cols)
    hin = jnp.dot(w_ref[0], u, preferred_element_type=F32)
    yr, yi = hin[:p], hin[p:]
    pos = lax.broadcasted_iota(jnp.int32, (p, cols), 1) % n_per_seq

    def shifted(a, d):
        return jnp.where(pos >= d, pltpu.roll(a, d, axis=1), 0.0)

    d, kk = 1, 0
    while d < n_per_seq:
        sr, si = shifted(yr, d), shifted(yi, d)
        ar, ai = apr_ref[0, kk], api_ref[0, kk]
        yr, yi = yr + ar * sr - ai * si, yi + ar * si + ai * sr
        d, kk = 2 * d, kk + 1
    h_prev = jnp.concatenate([shifted(yr, 1), shifted(yi, 1)], axis=0).astype(BF16)
    y = jnp.dot(m_ref[0], u, preferred_element_type=F32) + jnp.dot(v_ref[0], h_prev, preferred_element_type=F32)
    o_ref[0] = _gelu(y).reshape(rows, gs, cols).astype(o_ref.dtype)


def _s5_out_kernel(h_ref, w_ref, x_ref, g_ref, b_ref, o_ref, *, alpha):
    d = x_ref.shape[1]
    ht = h_ref[...].reshape(d, h_ref.shape[-1])
    vg = lax.dot_general(ht, w_ref[...], (((0,), (0,)), ((), ())), preferred_element_type=F32)
    o_ref[...] = _layer_norm(alpha * x_ref[...] + vg[:, :d] * jax.nn.sigmoid(vg[:, d:]), g_ref[...], b_ref[...])


def _s5_layer(x2, bsz, seq, w_in, lam_re, lam_im, log_dt, b_re, b_im, c_re, c_im, d_skip, w_glu, ln_g, ln_b, alpha):
    t, d = x2.shape
    n_per_seq = seq // CHUNK
    cols = t // CHUNK
    g, p = lam_re.shape
    gs = d // g
    m_mat, w_mat, v_mat, apr, api = _s5_tables(lam_re, lam_im, log_dt, b_re, b_im, c_re, c_im, d_skip, n_per_seq)
    xv = x2.reshape(cols, CHUNK * d)
    u_t = pl.pallas_call(
        _s5_in_kernel,
        grid=(CHUNK,),
        in_specs=[pl.BlockSpec((cols, d), lambda i: (0, i)), _const_spec((d, d))],
        out_specs=pl.BlockSpec((1, d, cols), lambda i: (i, 0, 0)),
        out_shape=jax.ShapeDtypeStruct((CHUNK, d, cols), BF16),
        compiler_params=_params("arbitrary"),
        name="s5_in",
    )(xv, w_in.T.astype(BF16))
    n_steps = apr.shape[1]
    hid = pl.pallas_call(
        functools.partial(_s5_scan_kernel, n_per_seq=n_per_seq),
        grid=(g,),
        in_specs=[pl.BlockSpec((CHUNK, gs, cols), lambda i: (0, i, 0)),
                  pl.BlockSpec((1, CHUNK * gs, CHUNK * gs), lambda i: (i, 0, 0)),
                  pl.BlockSpec((1, 2 * p, CHUNK * gs), lambda i: (i, 0, 0)),
                  pl.BlockSpec((1, CHUNK * gs, 2 * p), lambda i: (i, 0, 0)),
                  pl.BlockSpec((1, n_steps, p, 1), lambda i: (i, 0, 0, 0)),
                  pl.BlockSpec((1, n_steps, p, 1), lambda i: (i, 0, 0, 0))],
        out_specs=pl.BlockSpec((1, CHUNK, gs, cols), lambda i: (i, 0, 0, 0)),
        out_shape=jax.ShapeDtypeStruct((g, CHUNK, gs, cols), BF16),
        compiler_params=_params("arbitrary"),
        name="s5_scan",
    )(u_t, m_mat, w_mat, v_mat, apr, api)
    out = pl.pallas_call(
        functools.partial(_s5_out_kernel, alpha=alpha),
        grid=(CHUNK,),
        in_specs=[pl.BlockSpec((g, 1, gs, cols), lambda i: (0, i, 0, 0)), _const_spec(w_glu.shape),
                  pl.BlockSpec((cols, d), lambda i: (0, i)), _const_spec((1, d)), _const_spec((1, d))],
        out_specs=pl.BlockSpec((cols, d), lambda i: (0, i)),
        out_shape=jax.ShapeDtypeStruct((cols, CHUNK * d), F32),
        compiler_params=_params("arbitrary"),
        name="s5_out",
    )(hid, w_glu.astype(BF16), xv, ln_g.reshape(1, d), ln_b.reshape(1, d))
    return out.reshape(t, d)


def _top2_gates(logits):
    n_e = logits.shape[-1]
    lane = lax.broadcasted_iota(jnp.int32, logits.shape, 1)
    m1 = jnp.max(logits, axis=-1, keepdims=True)
    i1 = jnp.min(jnp.where(logits == m1, lane, n_e), axis=-1, keepdims=True)
    rest = jnp.where(lane == i1, -jnp.inf, logits)
    m2 = jnp.max(rest, axis=-1, keepdims=True)
    i2 = jnp.min(jnp.where(rest == m2, lane, n_e), axis=-1, keepdims=True)
    e2 = jnp.exp(m2 - m1)
    den = 1.0 + e2
    return i1, i2, 1.0 / den, e2 / den


def _moe_route_kernel(x_ref, wr_ref, br_ref, low_ref, idx_ref, gate_ref, cnt_ref, base_ref):
    @pl.when(pl.program_id(0) == 0)
    def _():
        base_ref[...] = jnp.zeros(base_ref.shape, F32)

    i1, i2, g1, g2 = _top2_gates(_dot_f32(x_ref[...], wr_ref[...]) + br_ref[...])
    lane = lax.broadcasted_iota(jnp.int32, (x_ref.shape[0], wr_ref.shape[1]), 1)
    sel = jnp.where((lane == i1) | (lane == i2), 1.0, 0.0)
    rank = base_ref[...] + jnp.dot(low_ref[...], sel.astype(BF16), preferred_element_type=F32)
    p1 = jnp.sum(jnp.where(lane == i1, rank, 0.0), axis=-1, keepdims=True).astype(jnp.int32)
    p2 = jnp.sum(jnp.where(lane == i2, rank, 0.0), axis=-1, keepdims=True).astype(jnp.int32)
    idx_ref[...] = jnp.where(lane == 0, i1, jnp.where(lane == 1, i2, jnp.where(lane == 2, p1, jnp.where(lane == 3, p2, 0))))
    gate_ref[...] = jnp.where(lane == 0, g1, jnp.where(lane == 1, g2, 0.0))
    base_ref[...] += jnp.sum(sel, axis=0, keepdims=True)
    cnt_ref[...] = base_ref[...].astype(jnp.int32)


def _row_gather_start(src_hbm, idx_ref, dst_ref, sem, n_rows):
    def body(r, carry):
        pltpu.make_async_copy(src_hbm.at[pl.ds(idx_ref[0, 0, r], 1)], dst_ref.at[pl.ds(r, 1)], sem).start()
        return carry
    lax.fori_loop(0, n_rows, body, 0, unroll=8)


def _row_gather_wait(src_hbm, dst_ref, sem, n_rows):
    pltpu.make_async_copy(src_hbm.at[pl.ds(0, n_rows)], dst_ref, sem).wait()


def _moe_expert_kernel(tile_e_ref, src_ref, src_next_ref, x_hbm, w1_ref, w3_ref, w2_ref, y_ref, xbuf, sem):
    del tile_e_ref
    i = pl.program_id(0)
    n = pl.num_programs(0)
    tr = xbuf.shape[1]
    slot = i % 2

    @pl.when(i == 0)
    def _():
        _row_gather_start(x_hbm, src_ref, xbuf.at[0], sem.at[0], tr)

    @pl.when(i + 1 < n)
    def _():
        _row_gather_start(x_hbm, src_next_ref, xbuf.at[1 - slot], sem.at[1 - slot], tr)

    _row_gather_wait(x_hbm, xbuf.at[slot], sem.at[slot], tr)
    xb = xbuf[slot].astype(BF16)
    hid = _silu(jnp.dot(xb, w1_ref[0], preferred_element_type=F32)) * jnp.dot(xb, w3_ref[0], preferred_element_type=F32)
    y_ref[...] = jnp.dot(hid.astype(BF16), w2_ref[0], preferred_element_type=F32)


def _moe_combine_kernel(d1_ref, d2_ref, d1n_ref, d2n_ref, x_ref, gate_ref, y_hbm, g_ref, b_ref, o_ref, ybuf, sem,
                        *, alpha):
    i = pl.program_id(0)
    n = pl.num_programs(0)
    tb = x_ref.shape[0]
    slot = i % 2

    def start(a_ref, b_ref, s):
        _row_gather_start(y_hbm, a_ref, ybuf.at[s, 0], sem.at[s], tb)
        _row_gather_start(y_hbm, b_ref, ybuf.at[s, 1], sem.at[s], tb)

    @pl.when(i == 0)
    def _():
        start(d1_ref, d2_ref, 0)

    @pl.when(i + 1 < n)
    def _():
        start(d1n_ref, d2n_ref, 1 - slot)

    _row_gather_wait(y_hbm, ybuf.at[slot, 0], sem.at[slot], tb)
    _row_gather_wait(y_hbm, ybuf.at[slot, 1], sem.at[slot], tb)
    gate = gate_ref[...]
    f = gate[:, 0:1] * ybuf[slot, 0] + gate[:, 1:2] * ybuf[slot, 1]
    o_ref[...] = _layer_norm(alpha * x_ref[...] + f, g_ref[...], b_ref[...])


def _moe_layer(x2, w_router, b_router, w1, w3, w2, ln_g, ln_b, alpha, tm=1024, tr=256, tb=256):
    t, d = x2.shape
    n_e, _, dff = w1.shape
    idx = jnp.arange(tm)
    low = (idx[:, None] > idx[None, :]).astype(BF16)
    route_i, route_g, counts = pl.pallas_call(
        _moe_route_kernel,
        grid=(t // tm,),
        in_specs=[pl.BlockSpec((tm, d), lambda i: (i, 0)), _const_spec((d, n_e)), _const_spec((1, n_e)),
                  _const_spec((tm, tm))],
        out_specs=[pl.BlockSpec((tm, n_e), lambda i: (i, 0)), pl.BlockSpec((tm, n_e), lambda i: (i, 0)),
                   pl.BlockSpec((1, n_e), lambda i: (0, 0))],
        out_shape=[jax.ShapeDtypeStruct((t, n_e), jnp.int32), jax.ShapeDtypeStruct((t, n_e), F32),
                   jax.ShapeDtypeStruct((1, n_e), jnp.int32)],
        scratch_shapes=[pltpu.VMEM((1, n_e), F32)],
        compiler_params=_params("arbitrary"),
        name="moe_route",
    )(x2, w_router.astype(F32), b_router.reshape(1, n_e).astype(F32), low)

    n_tiles = (2 * t) // tr + n_e
    cnt = counts[0]
    size = ((cnt + tr - 1) // tr) * tr
    ends = jnp.cumsum(size)
    offs = ends - size
    e1, e2, p1, p2 = route_i[:, 0], route_i[:, 1], route_i[:, 2], route_i[:, 3]
    dest1 = offs[e1] + p1
    dest2 = offs[e2] + p2
    tok = jnp.arange(t, dtype=jnp.int32)
    src = jnp.zeros((n_tiles * tr,), jnp.int32).at[dest1].set(tok).at[dest2].set(tok)
    tile_e = jnp.minimum(jnp.sum(jnp.arange(n_tiles, dtype=jnp.int32)[:, None] * tr >= ends[None, :], axis=1),
                         n_e - 1).astype(jnp.int32)
    src3 = src.reshape(n_tiles, 1, tr)

    smem_blk = lambda n: pl.BlockSpec((1, 1, n), lambda i, *_: (i, 0, 0), memory_space=pltpu.SMEM)
    smem_next = lambda n, last: pl.BlockSpec((1, 1, n), lambda i, *_: (jnp.minimum(i + 1, last), 0, 0),
                                             memory_space=pltpu.SMEM)
    wspec = lambda shape: pl.BlockSpec(shape, lambda i, te: (te[i], 0, 0))
    ys = pl.pallas_call(
        _moe_expert_kernel,
        grid_spec=pltpu.PrefetchScalarGridSpec(
            num_scalar_prefetch=1,
            grid=(n_tiles,),
            in_specs=[smem_blk(tr), smem_next(tr, n_tiles - 1), pl.BlockSpec(memory_space=pl.ANY),
                      wspec((1, d, dff)), wspec((1, d, dff)), wspec((1, dff, d))],
            out_specs=pl.BlockSpec((tr, d), lambda i, te: (i, 0)),
            scratch_shapes=[pltpu.VMEM((2, tr, d), F32), pltpu.SemaphoreType.DMA((2,))],
        ),
        out_shape=jax.ShapeDtypeStruct((n_tiles * tr, d), F32),
        compiler_params=_params("arbitrary"),
        name="moe_experts",
    )(tile_e, src3, src3, x2, w1.astype(BF16), w3.astype(BF16), w2.astype(BF16))

    n_tb = t // tb
    d1 = dest1.reshape(n_tb, 1, tb)
    d2 = dest2.reshape(n_tb, 1, tb)
    row = lambda i: (i, 0)
    return pl.pallas_call(
        functools.partial(_moe_combine_kernel, alpha=alpha),
        grid=(n_tb,),
        in_specs=[smem_blk(tb), smem_blk(tb), smem_next(tb, n_tb - 1), smem_next(tb, n_tb - 1),
                  pl.BlockSpec((tb, d), row), pl.BlockSpec((tb, n_e), row), pl.BlockSpec(memory_space=pl.ANY),
                  _const_spec((1, d)), _const_spec((1, d))],
        out_specs=pl.BlockSpec((tb, d), row),
        out_shape=jax.ShapeDtypeStruct((t, d), F32),
        scratch_shapes=[pltpu.VMEM((2, 2, tb, d), F32), pltpu.SemaphoreType.DMA((2,))],
        compiler_params=_params("arbitrary"),
        name="moe_combine",
    )(d1, d2, d1, d2, x2, route_g, ys, ln_g.reshape(1, d), ln_b.reshape(1, d))


def kernel(x, gdn_w_in, gdn_conv_w, gdn_a_log, gdn_dt_bias, gdn_norm_g, gdn_w_out, ffn_w1, ffn_w3, ffn_w2,
           s5_w_in, s5_lam_re, s5_lam_im, s5_log_dt, s5_b_re, s5_b_im, s5_c_re, s5_c_im, s5_d, s5_w_glu,
           moe_w_router, moe_b_router, moe_w1, moe_w3, moe_w2, ln_g, ln_b):
    bsz, seq, d = x.shape
    depth = ln_g.shape[0]
    alpha = (2 * depth) ** 0.25
    x2 = x.reshape(bsz * seq, d)
    for i in range(depth):
        j = i // 2
        if i % 2 == 0:
            x2 = _gdn_layer(x2, bsz, seq, gdn_w_in[j], gdn_conv_w[j], gdn_a_log[j], gdn_dt_bias[j], gdn_norm_g[j],
                            gdn_w_out[j], ln_g[i, 0], ln_b[i, 0], alpha)
            x2 = _ffn_ln(x2, ffn_w1[j], ffn_w3[j], ffn_w2[j], ln_g[i, 1], ln_b[i, 1], alpha)
        else:
            x2 = _s5_layer(x2, bsz, seq, s5_w_in[j], s5_lam_re[j], s5_lam_im[j], s5_log_dt[j], s5_b_re[j], s5_b_im[j],
                           s5_c_re[j], s5_c_im[j], s5_d[j], s5_w_glu[j], ln_g[i, 0], ln_b[i, 0], alpha)
            x2 = _moe_layer(x2, moe_w_router[j], moe_b_router[j], moe_w1[j], moe_w3[j], moe_w2[j],
                            ln_g[i, 1], ln_b[i, 1], alpha)
    return x2.reshape(bsz, seq, d)
```

```python
import functools
import math

import jax
import jax.numpy as jnp
from jax import lax
from jax.experimental import pallas as pl
from jax.experimental.pallas import tpu as pltpu

F32 = jnp.float32
BF16 = jnp.bfloat16

CHUNK = 64
GDN_HEADS = 8
GDN_DK = 128
GDN_CONV = 4
S5_GROUP = 16
S5_STATE = 64
S5_TBLK = 16
N_EXPERTS = 8
COMBINE_FRAMES = 8
LN_EPS = 1e-5
NORM_EPS = 1e-6

VMEM_LIMIT_BYTES = 56 * 1024 * 1024
CONV_CARRY_ROWS = 8
MXU_COLS = 256


def _params(*semantics):
    return pltpu.CompilerParams(dimension_semantics=semantics, vmem_limit_bytes=VMEM_LIMIT_BYTES)


def _const_spec(shape):
    nd = len(shape)
    return pl.BlockSpec(shape, lambda *_: (0,) * nd, pipeline_mode=pl.Buffered(1))


def _dot(a, b):
    return jnp.dot(a.astype(BF16), b.astype(BF16), preferred_element_type=F32)


def _dot_nt(a, b):
    return lax.dot_general(a.astype(BF16), b.astype(BF16), (((1,), (1,)), ((), ())),
                           preferred_element_type=F32)


def _dot_tn(a, b):
    return lax.dot_general(a.astype(BF16), b.astype(BF16), (((0,), (0,)), ((), ())),
                           preferred_element_type=F32)


def _dot_f32(a, b):
    return jnp.dot(a, b, preferred_element_type=F32, precision=lax.Precision.HIGHEST)


def _silu(x):
    return x * jax.nn.sigmoid(x)


def _gelu(x):
    return 0.5 * x * (1.0 + lax.erf(x * (2.0 ** -0.5)))


def _softplus(x):
    return jnp.maximum(x, 0.0) + jnp.log1p(jnp.exp(-jnp.abs(x)))


def _layer_norm(y, g, b):
    mu = jnp.mean(y, axis=-1, keepdims=True)
    yc = y - mu
    var = jnp.mean(yc * yc, axis=-1, keepdims=True)
    return yc * lax.rsqrt(var + LN_EPS) * g + b


def _gdn_in_kernel(x_ref, wqkv_ref, wz_ref, wb_ref, wa_ref, waT_ref, conv_ref, alog_ref, dtb_ref,
                   alogc_ref, dtbc_ref, tri_ref, triT_ref,
                   q_ref, k_ref, v_ref, z_ref, beta_ref, gc_ref, gcT_ref, carry_ref,
                   *, tiles_per_seq, tm, n_qk):
    i = pl.program_id(0)

    @pl.when(i % tiles_per_seq == 0)
    def _():
        carry_ref[...] = jnp.zeros(carry_ref.shape, F32)

    xb = x_ref[...].astype(BF16)
    z_ref[...] = jnp.dot(xb, wz_ref[...], preferred_element_type=F32).astype(z_ref.dtype)
    beta_ref[...] = jax.nn.sigmoid(jnp.dot(xb, wb_ref[...], preferred_element_type=F32))

    g_col = -jnp.exp(alog_ref[...]) * _softplus(jnp.dot(xb, wa_ref[...], preferred_element_type=F32) + dtb_ref[...])
    for c in range(tm // CHUNK):
        rows = slice(c * CHUNK, (c + 1) * CHUNK)
        gc_ref[rows, :] = _dot_f32(tri_ref[...], g_col[rows, :])
        a_row = lax.dot_general(waT_ref[...], xb[rows, :], (((1,), (1,)), ((), ())), preferred_element_type=F32)
        g_row = -jnp.exp(alogc_ref[...]) * _softplus(a_row + dtbc_ref[...])
        gcT_ref[c] = _dot_f32(g_row, triT_ref[...])

    for blk in range(carry_ref.shape[1] // MXU_COLS):
        cols = slice(blk * MXU_COLS, (blk + 1) * MXU_COLS)
        p = jnp.dot(xb, wqkv_ref[:, cols], preferred_element_type=F32)
        ext = jnp.concatenate([carry_ref[:, cols], p], axis=0)
        carry_ref[:, cols] = p[tm - CONV_CARRY_ROWS:tm, :]
        acc = conv_ref[GDN_CONV - 1:GDN_CONV, cols] * p
        for j in range(1, GDN_CONV):
            shifted = pltpu.roll(ext, j, axis=0)[CONV_CARRY_ROWS:, :]
            acc = acc + conv_ref[GDN_CONV - 1 - j:GDN_CONV - j, cols] * shifted
        y = _silu(acc)
        for hh in range(MXU_COLS // GDN_DK):
            head = blk * (MXU_COLS // GDN_DK) + hh
            yh = y[:, hh * GDN_DK:(hh + 1) * GDN_DK]
            if head < 2 * n_qk:
                yh = yh * lax.rsqrt(jnp.sum(yh * yh, axis=-1, keepdims=True) + NORM_EPS)
            if head < n_qk:
                q_ref[:, head * GDN_DK:(head + 1) * GDN_DK] = (yh * (GDN_DK ** -0.5)).astype(q_ref.dtype)
            elif head < 2 * n_qk:
                k_ref[:, (head - n_qk) * GDN_DK:(head - n_qk + 1) * GDN_DK] = yh.astype(k_ref.dtype)
            else:
                v_ref[:, (head - 2 * n_qk) * GDN_DK:(head - 2 * n_qk + 1) * GDN_DK] = yh.astype(v_ref.dtype)


def _gdn_in(x2, w_in, conv_w, a_log, dt_bias, seq, tm=256):
    t, d = x2.shape
    tm = min(tm, seq)
    nh = GDN_HEADS
    qk = nh * GDN_DK
    dv = (w_in.shape[1] - 2 * qk - 2 * nh) // 2
    wqkv = w_in[:, :2 * qk + dv].astype(BF16)
    wz = w_in[:, 2 * qk + dv:2 * qk + 2 * dv].astype(BF16)
    wb = w_in[:, 2 * qk + 2 * dv:2 * qk + 2 * dv + nh].astype(BF16)
    wa = w_in[:, 2 * qk + 2 * dv + nh:].astype(BF16)
    idx = jnp.arange(CHUNK)
    tri = (idx[:, None] >= idx[None, :]).astype(F32)
    n_chunks = t // CHUNK
    kern = functools.partial(_gdn_in_kernel, tiles_per_seq=seq // tm, tm=tm, n_qk=nh)
    row = lambda i: (i, 0)
    return pl.pallas_call(
        kern,
        grid=(t // tm,),
        in_specs=[
            pl.BlockSpec((tm, d), row),
            _const_spec(wqkv.shape), _const_spec(wz.shape), _const_spec(wb.shape), _const_spec(wa.shape),
            _const_spec((nh, d)), _const_spec(conv_w.shape),
            _const_spec((1, nh)), _const_spec((1, nh)), _const_spec((nh, 1)), _const_spec((nh, 1)),
            _const_spec((CHUNK, CHUNK)), _const_spec((CHUNK, CHUNK)),
        ],
        out_specs=[
            pl.BlockSpec((tm, qk), row), pl.BlockSpec((tm, qk), row), pl.BlockSpec((tm, dv), row),
            pl.BlockSpec((tm, dv), row), pl.BlockSpec((tm, nh), row), pl.BlockSpec((tm, nh), row),
            pl.BlockSpec((tm // CHUNK, nh, CHUNK), lambda i: (i, 0, 0)),
        ],
        out_shape=[
            jax.ShapeDtypeStruct((t, qk), BF16), jax.ShapeDtypeStruct((t, qk), BF16),
            jax.ShapeDtypeStruct((t, dv), BF16), jax.ShapeDtypeStruct((t, dv), BF16),
            jax.ShapeDtypeStruct((t, nh), F32), jax.ShapeDtypeStruct((t, nh), F32),
            jax.ShapeDtypeStruct((n_chunks, nh, CHUNK), F32),
        ],
        scratch_shapes=[pltpu.VMEM((CONV_CARRY_ROWS, 2 * qk + dv), F32)],
        compiler_params=_params("arbitrary"),
        name="gdn_in",
    )(x2, wqkv, wz, wb, wa, wa.T, conv_w.astype(F32),
      a_log.reshape(1, nh).astype(F32), dt_bias.reshape(1, nh).astype(F32),
      a_log.reshape(nh, 1).astype(F32), dt_bias.reshape(nh, 1).astype(F32), tri, tri.T)


def _gdn_core_kernel(q_ref, k_ref, v_ref, z_ref, beta_ref, gc_ref, gcT_ref, ng_ref, o_ref,
                     s_ref, u_ref, wq_ref, kd_ref, a_ref, *, n_sub):
    nh = s_ref.shape[0]
    heads = range(nh)
    cols = [slice(h * GDN_DK, (h + 1) * GDN_DK) for h in heads]

    @pl.when(pl.program_id(1) == 0)
    def _():
        s_ref[...] = jnp.zeros(s_ref.shape, F32)

    r = lax.broadcasted_iota(jnp.int32, (CHUNK, CHUNK), 0)
    c = lax.broadcasted_iota(jnp.int32, (CHUNK, CHUNK), 1)
    causal = r >= c
    strict = r > c
    eye = jnp.where(r == c, 1.0, 0.0)

    def prep(ci, carry):
        rows = pl.ds(pl.multiple_of(ci * CHUNK, CHUNK), CHUNK)
        beta = beta_ref[rows, :]
        gc = gc_ref[rows, :]
        gct = gcT_ref[ci]
        q = [q_ref[rows, cols[h]].astype(F32) for h in heads]
        k = [k_ref[rows, cols[h]].astype(F32) for h in heads]
        g_col = [gc[:, h:h + 1] for h in heads]
        b_col = [beta[:, h:h + 1] for h in heads]
        decay = [jnp.exp(jnp.where(causal, g_col[h] - gct[h:h + 1, :], -jnp.inf)) for h in heads]
        eg = [jnp.exp(g_col[h]) for h in heads]
        kb = [k[h] * b_col[h] for h in heads]
        m = [jnp.where(strict, _dot_nt(kb[h], k[h]) * decay[h], 0.0) for h in heads]
        qk = [_dot_nt(q[h], k[h]) for h in heads]
        p = [eye - m[h] for h in heads]
        step = 1
        while 2 * step <= CHUNK // 2:
            m = [_dot(m[h], m[h]) for h in heads]
            p = [p[h] + _dot(p[h], m[h]) for h in heads]
            step *= 2
        rhs = [jnp.concatenate([v_ref[rows, cols[h]].astype(F32) * b_col[h], kb[h] * eg[h]], axis=1) for h in heads]
        uw = [_dot(p[h], rhs[h]) for h in heads]
        g_last = gc[CHUNK - 1:CHUNK, :]
        for h in heads:
            u_ref[ci, h] = uw[h][:, :GDN_DK]
            wq_ref[ci, h, 0:CHUNK, :] = uw[h][:, GDN_DK:].astype(BF16)
            wq_ref[ci, h, CHUNK:2 * CHUNK, :] = (q[h] * eg[h]).astype(BF16)
            kd_ref[ci, h] = (k[h] * jnp.exp(g_last[:, h:h + 1] - g_col[h])).astype(BF16)
            a_ref[ci, h] = (qk[h] * decay[h]).astype(BF16)
        return carry

    lax.fori_loop(0, n_sub, prep, 0)

    def scan(ci, carry):
        rows = pl.ds(pl.multiple_of(ci * CHUNK, CHUNK), CHUNK)
        e_last = jnp.exp(gc_ref[pl.ds(ci * CHUNK + CHUNK - 1, 1), :])
        s = [s_ref[h] for h in heads]
        ws = [jnp.dot(wq_ref[ci, h], s[h].astype(BF16), preferred_element_type=F32) for h in heads]
        v_new = [(u_ref[ci, h] - ws[h][:CHUNK]).astype(BF16) for h in heads]
        o = [ws[h][CHUNK:] + jnp.dot(a_ref[ci, h], v_new[h], preferred_element_type=F32) for h in heads]
        upd = [lax.dot_general(kd_ref[ci, h], v_new[h], (((0,), (0,)), ((), ())), preferred_element_type=F32)
               for h in heads]
        for h in heads:
            s_ref[h] = s[h] * e_last[:, h:h + 1] + upd[h]
            on = o[h] * lax.rsqrt(jnp.mean(o[h] * o[h], axis=-1, keepdims=True) + NORM_EPS) * ng_ref[...]
            o_ref[rows, cols[h]] = (on * _silu(z_ref[rows, cols[h]].astype(F32))).astype(o_ref.dtype)
        return carry

    lax.fori_loop(0, n_sub, scan, 0)


def _gdn_core(q, k, v, z, beta, gc, gct, norm_g, bsz, seq, tc=512):
    t, qk = q.shape
    tc = min(tc, seq)
    nh = GDN_HEADS
    dv = v.shape[1]
    tiles = seq // tc
    n_sub = tc // CHUNK
    row = lambda b, i: (b * tiles + i, 0)
    return pl.pallas_call(
        functools.partial(_gdn_core_kernel, n_sub=n_sub),
        grid=(bsz, tiles),
        in_specs=[
            pl.BlockSpec((tc, qk), row), pl.BlockSpec((tc, qk), row), pl.BlockSpec((tc, dv), row),
            pl.BlockSpec((tc, dv), row), pl.BlockSpec((tc, nh), row), pl.BlockSpec((tc, nh), row),
            pl.BlockSpec((n_sub, nh, CHUNK), lambda b, i: (b * tiles + i, 0, 0)),
            _const_spec((1, dv // nh)),
        ],
        out_specs=pl.BlockSpec((tc, dv), row),
        out_shape=jax.ShapeDtypeStruct((t, dv), BF16),
        scratch_shapes=[pltpu.VMEM((nh, GDN_DK, dv // nh), F32),
                        pltpu.VMEM((n_sub, nh, CHUNK, dv // nh), F32),
                        pltpu.VMEM((n_sub, nh, 2 * CHUNK, GDN_DK), BF16),
                        pltpu.VMEM((n_sub, nh, CHUNK, GDN_DK), BF16),
                        pltpu.VMEM((n_sub, nh, CHUNK, CHUNK), BF16)],
        compiler_params=_params("arbitrary", "arbitrary"),
        name="gdn_core",
    )(q, k, v, z, beta, gc, gct, norm_g.reshape(1, -1).astype(F32))


def _proj_ln_kernel(a_ref, w_ref, x_ref, g_ref, b_ref, o_ref, *, alpha):
    h = jnp.dot(a_ref[...].astype(BF16), w_ref[...], preferred_element_type=F32)
    o_ref[...] = _layer_norm(alpha * x_ref[...] + h, g_ref[...], b_ref[...])


def _proj_ln(a, w, x2, ln_g, ln_b, alpha, tm=512):
    t, d = x2.shape
    tm = min(tm, t)
    row = lambda i: (i, 0)
    return pl.pallas_call(
        functools.partial(_proj_ln_kernel, alpha=alpha),
        grid=(t // tm,),
        in_specs=[pl.BlockSpec((tm, a.shape[1]), row), _const_spec(w.shape), pl.BlockSpec((tm, d), row),
                  _const_spec((1, d)), _const_spec((1, d))],
        out_specs=pl.BlockSpec((tm, d), row),
        out_shape=jax.ShapeDtypeStruct((t, d), F32),
        compiler_params=_params("arbitrary"),
        name="proj_ln",
    )(a, w.astype(BF16), x2, ln_g.reshape(1, d), ln_b.reshape(1, d))


def _ffn_ln_kernel(x_ref, w1_ref, w3_ref, w2_ref, g_ref, b_ref, o_ref, *, alpha, chunk_major):
    x = x_ref[...]
    xb = x.astype(BF16)
    hid = _silu(jnp.dot(xb, w1_ref[...], preferred_element_type=F32)) * jnp.dot(xb, w3_ref[...], preferred_element_type=F32)
    f = jnp.dot(hid.astype(BF16), w2_ref[...], preferred_element_type=F32)
    y = _layer_norm(alpha * x + f, g_ref[...], b_ref[...])
    if chunk_major:
        for c in range(o_ref.shape[1]):
            o_ref[:, c, :] = y[c * CHUNK:(c + 1) * CHUNK, :]
    else:
        o_ref[...] = y


def _ffn_ln(x2, w1, w3, w2, ln_g, ln_b, alpha, chunk_major=False, tm=512):
    t, d = x2.shape
    tm = min(tm, t)
    row = lambda i: (i, 0)
    if chunk_major:
        out_spec = pl.BlockSpec((CHUNK, tm // CHUNK, d), lambda i: (0, i, 0))
        out_shape = jax.ShapeDtypeStruct((CHUNK, t // CHUNK, d), F32)
    else:
        out_spec = pl.BlockSpec((tm, d), row)
        out_shape = jax.ShapeDtypeStruct((t, d), F32)
    return pl.pallas_call(
        functools.partial(_ffn_ln_kernel, alpha=alpha, chunk_major=chunk_major),
        grid=(t // tm,),
        in_specs=[pl.BlockSpec((tm, d), row), _const_spec(w1.shape), _const_spec(w3.shape), _const_spec(w2.shape),
                  _const_spec((1, d)), _const_spec((1, d))],
        out_specs=out_spec,
        out_shape=out_shape,
        compiler_params=_params("arbitrary"),
        name="ffn_ln",
    )(x2, w1.astype(BF16), w3.astype(BF16), w2.astype(BF16), ln_g.reshape(1, d), ln_b.reshape(1, d))


def _gdn_layer(x2, bsz, seq, w_in, conv_w, a_log, dt_bias, norm_g, w_out, ln_g, ln_b, alpha):
    q, k, v, z, beta, gc, gct = _gdn_in(x2, w_in, conv_w, a_log, dt_bias, seq)
    o = _gdn_core(q, k, v, z, beta, gc, gct, norm_g, bsz, seq)
    return _proj_ln(o, w_out, x2, ln_g, ln_b, alpha)


def _s5_taps_kernel(ce_ref, b_ref, o_ref):
    o_ref[0] = _dot_f32(ce_ref[0], b_ref[0])


def _s5_tables(lam_re, lam_im, log_dt, b_re, b_im, c_re, c_im, d_skip, n_per_seq):
    g, p = lam_re.shape
    gs = b_re.shape[-1]
    lr, li = lam_re.astype(F32), lam_im.astype(F32)
    dt = jnp.exp(log_dt.astype(F32))[:, None]
    zr, zi = lr * dt, li * dt

    def a_pow(steps):
        s = steps.astype(F32)[:, None, None]
        mag = jnp.exp(zr[None] * s)
        return mag * jnp.cos(zi[None] * s), mag * jnp.sin(zi[None] * s)

    ar, ai = a_pow(jnp.arange(CHUNK + 1))
    nr, ni = ar[1] - 1.0, ai[1]
    den = lr * lr + li * li
    qr, qi = (nr * lr + ni * li) / den, (ni * lr - nr * li) / den
    bbr = qr[..., None] * b_re - qi[..., None] * b_im
    bbi = qr[..., None] * b_im + qi[..., None] * b_re
    cr, ci = c_re.astype(F32), c_im.astype(F32)
    car = cr[:, None] * jnp.moveaxis(ar, 0, 1)[:, :, None, :] - ci[:, None] * jnp.moveaxis(ai, 0, 1)[:, :, None, :]
    cai = cr[:, None] * jnp.moveaxis(ai, 0, 1)[:, :, None, :] + ci[:, None] * jnp.moveaxis(ar, 0, 1)[:, :, None, :]
    ca = jnp.concatenate([car, -cai], axis=-1)
    ce = ca[:, :CHUNK].reshape(g, CHUNK * gs, 2 * p)
    bst = jnp.concatenate([bbr, bbi], axis=1)
    taps = pl.pallas_call(
        _s5_taps_kernel,
        grid=(g,),
        in_specs=[pl.BlockSpec((1, CHUNK * gs, 2 * p), lambda i: (i, 0, 0)),
                  pl.BlockSpec((1, 2 * p, gs), lambda i: (i, 0, 0))],
        out_specs=pl.BlockSpec((1, CHUNK * gs, gs), lambda i: (i, 0, 0)),
        out_shape=jax.ShapeDtypeStruct((g, CHUNK * gs, gs), F32),
        compiler_params=_params("arbitrary"),
        name="s5_taps",
    )(ce, bst).reshape(g, CHUNK, gs, gs)
    taps = taps.at[:, 0].add(d_skip.astype(F32).reshape(g, gs)[:, :, None] * jnp.eye(gs, dtype=F32))
    nb = CHUNK // S5_TBLK
    blk = jnp.arange(S5_TBLK)
    lag = S5_TBLK * jnp.arange(nb)[:, None, None] + blk[None, :, None] - blk[None, None, :]
    stk = jnp.where((lag >= 0)[None, :, :, :, None, None], taps[:, jnp.clip(lag, 0, CHUNK - 1)], 0.0)
    m_mat = jnp.transpose(stk, (0, 1, 2, 4, 3, 5)).reshape(g, CHUNK * gs, S5_TBLK * gs).astype(BF16)
    arr, aii = jnp.moveaxis(ar[:CHUNK][::-1], 0, 1), jnp.moveaxis(ai[:CHUNK][::-1], 0, 1)
    wr = arr[:, :, :, None] * bbr[:, None] - aii[:, :, :, None] * bbi[:, None]
    wi = arr[:, :, :, None] * bbi[:, None] + aii[:, :, :, None] * bbr[:, None]
    w_mat = jnp.concatenate([jnp.transpose(wr, (0, 2, 1, 3)), jnp.transpose(wi, (0, 2, 1, 3))], axis=1)
    w_mat = w_mat.reshape(g, 2 * p, CHUNK * gs).astype(BF16)
    v_mat = ca[:, 1:].reshape(g, CHUNK * gs, 2 * p).astype(BF16)
    n_steps = max(1, (n_per_seq - 1).bit_length())
    apr, api = a_pow(CHUNK * 2 ** jnp.arange(n_steps))
    apr = jnp.moveaxis(apr, 0, 1)[..., None]
    api = jnp.moveaxis(api, 0, 1)[..., None]
    return m_mat, w_mat, v_mat, apr, api


def _s5_in_kernel(x_ref, wT_ref, o_ref):
    o_ref[0] = _dot_nt(wT_ref[...], x_ref[0]).astype(o_ref.dtype)


def _s5_scan_kernel(u_ref, m_ref, w_ref, v_ref, apr_ref, api_ref, o_ref, *, n_per_seq):
    rows, gs, cols = u_ref.shape
    p = apr_ref.shape[2]
    u = u_ref[...].reshape(rows * gs, cols)
    hin = jnp.dot(w_ref[0], u, preferred_element_type=F32)
    yr, yi = hin[:p], hin[p:]
    pos = lax.broadcasted_iota(jnp.int32, (p, cols), 1) % n_per_seq

    def shifted(a, d):
        return jnp.where(pos >= d, pltpu.roll(a, d, axis=1), 0.0)

    d, kk = 1, 0
    while d < n_per_seq:
        sr, si = shifted(yr, d), shifted(yi, d)
        ar, ai = apr_ref[0, kk], api_ref[0, kk]
        yr, yi = yr + ar * sr - ai * si, yi + ar * si + ai * sr
        d, kk = 2 * d, kk + 1
    h_prev = jnp.concatenate([shifted(yr, 1), shifted(yi, 1)], axis=0).astype(BF16)
    nb = rows // S5_TBLK
    br = S5_TBLK * gs
    ys = [jnp.dot(v_ref[0, b * br:(b + 1) * br, :], h_prev, preferred_element_type=F32) for b in range(nb)]
    for sb in range(nb):
        part = jnp.dot(m_ref[0, 0:(nb - sb) * br, :], u[sb * br:(sb + 1) * br, :], preferred_element_type=F32)
        for j in range(nb - sb):
            ys[sb + j] = ys[sb + j] + part[j * br:(j + 1) * br, :]
    y = jnp.concatenate(ys, axis=0)
    o_ref[0] = _gelu(y).reshape(rows, gs, cols).astype(o_ref.dtype)


def _s5_out_kernel(h_ref, w_ref, x_ref, g_ref, b_ref, o_ref, *, alpha):
    d = x_ref.shape[-1]
    ht = h_ref[...].reshape(d, h_ref.shape[-1])
    vg = lax.dot_general(ht, w_ref[...], (((0,), (0,)), ((), ())), preferred_element_type=F32)
    o_ref[0] = _layer_norm(alpha * x_ref[0] + vg[:, :d] * jax.nn.sigmoid(vg[:, d:]), g_ref[...], b_ref[...])


def _s5_layer(xt, seq, w_in, lam_re, lam_im, log_dt, b_re, b_im, c_re, c_im, d_skip, w_glu, ln_g, ln_b, alpha):
    _, cols, d = xt.shape
    n_per_seq = seq // CHUNK
    g, p = lam_re.shape
    gs = d // g
    m_mat, w_mat, v_mat, apr, api = _s5_tables(lam_re, lam_im, log_dt, b_re, b_im, c_re, c_im, d_skip, n_per_seq)
    u_t = pl.pallas_call(
        _s5_in_kernel,
        grid=(CHUNK,),
        in_specs=[pl.BlockSpec((1, cols, d), lambda i: (i, 0, 0)), _const_spec((d, d))],
        out_specs=pl.BlockSpec((1, d, cols), lambda i: (i, 0, 0)),
        out_shape=jax.ShapeDtypeStruct((CHUNK, d, cols), BF16),
        compiler_params=_params("arbitrary"),
        name="s5_in",
    )(xt, w_in.T.astype(BF16))
    n_steps = apr.shape[1]
    hid = pl.pallas_call(
        functools.partial(_s5_scan_kernel, n_per_seq=n_per_seq),
        grid=(g,),
        in_specs=[pl.BlockSpec((CHUNK, gs, cols), lambda i: (0, i, 0)),
                  pl.BlockSpec((1, CHUNK * gs, S5_TBLK * gs), lambda i: (i, 0, 0)),
                  pl.BlockSpec((1, 2 * p, CHUNK * gs), lambda i: (i, 0, 0)),
                  pl.BlockSpec((1, CHUNK * gs, 2 * p), lambda i: (i, 0, 0)),
                  pl.BlockSpec((1, n_steps, p, 1), lambda i: (i, 0, 0, 0)),
                  pl.BlockSpec((1, n_steps, p, 1), lambda i: (i, 0, 0, 0))],
        out_specs=pl.BlockSpec((1, CHUNK, gs, cols), lambda i: (i, 0, 0, 0)),
        out_shape=jax.ShapeDtypeStruct((g, CHUNK, gs, cols), BF16),
        compiler_params=_params("arbitrary"),
        name="s5_scan",
    )(u_t, m_mat, w_mat, v_mat, apr, api)
    return pl.pallas_call(
        functools.partial(_s5_out_kernel, alpha=alpha),
        grid=(CHUNK,),
        in_specs=[pl.BlockSpec((g, 1, gs, cols), lambda i: (0, i, 0, 0)), _const_spec(w_glu.shape),
                  pl.BlockSpec((1, cols, d), lambda i: (i, 0, 0)), _const_spec((1, d)), _const_spec((1, d))],
        out_specs=pl.BlockSpec((1, cols, d), lambda i: (i, 0, 0)),
        out_shape=jax.ShapeDtypeStruct((CHUNK, cols, d), F32),
        compiler_params=_params("arbitrary"),
        name="s5_out",
    )(hid, w_glu.astype(BF16), xt, ln_g.reshape(1, d), ln_b.reshape(1, d))


def _top2_gates(logits):
    n_e = logits.shape[-1]
    lane = lax.broadcasted_iota(jnp.int32, logits.shape, 1)
    m1 = jnp.max(logits, axis=-1, keepdims=True)
    i1 = jnp.min(jnp.where(logits == m1, lane, n_e), axis=-1, keepdims=True)
    rest = jnp.where(lane == i1, -jnp.inf, logits)
    m2 = jnp.max(rest, axis=-1, keepdims=True)
    i2 = jnp.min(jnp.where(rest == m2, lane, n_e), axis=-1, keepdims=True)
    e2 = jnp.exp(m2 - m1)
    den = 1.0 + e2
    return i1, i2, 1.0 / den, e2 / den


def _moe_route_kernel(x_ref, wr_ref, br_ref, low_ref, idx_ref, gate_ref, cnt_ref, base_ref):
    @pl.when(pl.program_id(0) == 0)
    def _():
        base_ref[...] = jnp.zeros(base_ref.shape, F32)

    i1, i2, g1, g2 = _top2_gates(_dot_f32(x_ref[...], wr_ref[...]) + br_ref[...])
    lane = lax.broadcasted_iota(jnp.int32, (x_ref.shape[0], wr_ref.shape[1]), 1)
    sel = jnp.where((lane == i1) | (lane == i2), 1.0, 0.0)
    rank = base_ref[...] + jnp.dot(low_ref[...], sel.astype(BF16), preferred_element_type=F32)
    p1 = jnp.sum(jnp.where(lane == i1, rank, 0.0), axis=-1, keepdims=True).astype(jnp.int32)
    p2 = jnp.sum(jnp.where(lane == i2, rank, 0.0), axis=-1, keepdims=True).astype(jnp.int32)
    idx_ref[...] = jnp.where(lane == 0, i1, jnp.where(lane == 1, i2, jnp.where(lane == 2, p1, jnp.where(lane == 3, p2, 0))))
    gate_ref[...] = jnp.where(lane == 0, g1, jnp.where(lane == 1, g2, 0.0))
    base_ref[...] += jnp.sum(sel, axis=0, keepdims=True)
    cnt_ref[...] = base_ref[...].astype(jnp.int32)


def _row_gather_start(src_hbm, idx_ref, dst_ref, sem, n_rows):
    def body(r, carry):
        pltpu.make_async_copy(src_hbm.at[pl.ds(idx_ref[0, 0, r], 1)], dst_ref.at[pl.ds(r, 1)], sem).start()
        return carry
    lax.fori_loop(0, n_rows, body, 0, unroll=8)


def _row_gather_wait(src_hbm, dst_ref, sem, n_rows):
    pltpu.make_async_copy(src_hbm.at[pl.ds(0, n_rows)], dst_ref, sem).wait()


def _moe_expert_kernel(tile_e_ref, src_ref, src_next_ref, x_hbm, w1_ref, w3_ref, w2_ref, y_ref, xbuf, sem):
    del tile_e_ref
    i = pl.program_id(0)
    n = pl.num_programs(0)
    tr = xbuf.shape[1]
    slot = i % 2

    @pl.when(i == 0)
    def _():
        _row_gather_start(x_hbm, src_ref, xbuf.at[0], sem.at[0], tr)

    @pl.when(i + 1 < n)
    def _():
        _row_gather_start(x_hbm, src_next_ref, xbuf.at[1 - slot], sem.at[1 - slot], tr)

    _row_gather_wait(x_hbm, xbuf.at[slot], sem.at[slot], tr)
    xb = xbuf[slot].astype(BF16)
    hid = _silu(jnp.dot(xb, w1_ref[0], preferred_element_type=F32)) * jnp.dot(xb, w3_ref[0], preferred_element_type=F32)
    y_ref[...] = jnp.dot(hid.astype(BF16), w2_ref[0], preferred_element_type=F32)


def _moe_combine_kernel(d1_ref, d2_ref, d1n_ref, d2n_ref, x_ref, gate_ref, y_hbm, g_ref, b_ref, o_ref, ybuf, sem,
                        *, alpha):
    i = pl.program_id(0)
    n = pl.num_programs(0)
    nf, cb, d = x_ref.shape
    tb = nf * cb
    slot = i % 2

    def start(a_ref, b_ref, s):
        _row_gather_start(y_hbm, a_ref, ybuf.at[s, 0], sem.at[s], tb)
        _row_gather_start(y_hbm, b_ref, ybuf.at[s, 1], sem.at[s], tb)

    @pl.when(i == 0)
    def _():
        start(d1_ref, d2_ref, 0)

    @pl.when(i + 1 < n)
    def _():
        start(d1n_ref, d2n_ref, 1 - slot)

    _row_gather_wait(y_hbm, ybuf.at[slot, 0], sem.at[slot], tb)
    _row_gather_wait(y_hbm, ybuf.at[slot, 1], sem.at[slot], tb)
    gate = gate_ref[...].reshape(tb, gate_ref.shape[-1])
    f = gate[:, 0:1] * ybuf[slot, 0] + gate[:, 1:2] * ybuf[slot, 1]
    y = _layer_norm(alpha * x_ref[...].reshape(tb, d) + f, g_ref[...], b_ref[...])
    for k in range(nf):
        o_ref[:, k, :] = y[k * cb:(k + 1) * cb, :]


def _moe_layer(xt, w_router, b_router, w1, w3, w2, ln_g, ln_b, alpha, tm=1024, tr=256, tb=256):
    n_f, n_c, d = xt.shape
    t = n_f * n_c
    x2 = xt.reshape(t, d)
    n_e, _, dff = w1.shape
    tm = min(tm, t)
    tb = min(tb, COMBINE_FRAMES * n_c)
    idx = jnp.arange(tm)
    low = (idx[:, None] > idx[None, :]).astype(BF16)
    route_i, route_g, counts = pl.pallas_call(
        _moe_route_kernel,
        grid=(t // tm,),
        in_specs=[pl.BlockSpec((tm, d), lambda i: (i, 0)), _const_spec((d, n_e)), _const_spec((1, n_e)),
                  _const_spec((tm, tm))],
        out_specs=[pl.BlockSpec((tm, n_e), lambda i: (i, 0)), pl.BlockSpec((tm, n_e), lambda i: (i, 0)),
                   pl.BlockSpec((1, n_e), lambda i: (0, 0))],
        out_shape=[jax.ShapeDtypeStruct((t, n_e), jnp.int32), jax.ShapeDtypeStruct((t, n_e), F32),
                   jax.ShapeDtypeStruct((1, n_e), jnp.int32)],
        scratch_shapes=[pltpu.VMEM((1, n_e), F32)],
        compiler_params=_params("arbitrary"),
        name="moe_route",
    )(x2, w_router.astype(F32), b_router.reshape(1, n_e).astype(F32), low)

    n_tiles = (2 * t) // tr + n_e
    cnt = counts[0]
    size = ((cnt + tr - 1) // tr) * tr
    ends = jnp.cumsum(size)
    offs = ends - size
    e1, e2, p1, p2 = route_i[:, 0], route_i[:, 1], route_i[:, 2], route_i[:, 3]
    dest1 = offs[e1] + p1
    dest2 = offs[e2] + p2
    tok = jnp.arange(t, dtype=jnp.int32)
    src = jnp.zeros((n_tiles * tr,), jnp.int32).at[jnp.concatenate([dest1, dest2])].set(jnp.concatenate([tok, tok]))
    tile_e = jnp.minimum(jnp.sum(jnp.arange(n_tiles, dtype=jnp.int32)[:, None] * tr >= ends[None, :], axis=1),
                         n_e - 1).astype(jnp.int32)
    src3 = src.reshape(n_tiles, 1, tr)

    smem_blk = lambda n: pl.BlockSpec((1, 1, n), lambda i, *_: (i, 0, 0), memory_space=pltpu.SMEM)
    smem_next = lambda n, last: pl.BlockSpec((1, 1, n), lambda i, *_: (jnp.minimum(i + 1, last), 0, 0),
                                             memory_space=pltpu.SMEM)
    wspec = lambda shape: pl.BlockSpec(shape, lambda i, te: (te[i], 0, 0))
    ys = pl.pallas_call(
        _moe_expert_kernel,
        grid_spec=pltpu.PrefetchScalarGridSpec(
            num_scalar_prefetch=1,
            grid=(n_tiles,),
            in_specs=[smem_blk(tr), smem_next(tr, n_tiles - 1), pl.BlockSpec(memory_space=pl.ANY),
                      wspec((1, d, dff)), wspec((1, d, dff)), wspec((1, dff, d))],
            out_specs=pl.BlockSpec((tr, d), lambda i, te: (i, 0)),
            scratch_shapes=[pltpu.VMEM((2, tr, d), F32), pltpu.SemaphoreType.DMA((2,))],
        ),
        out_shape=jax.ShapeDtypeStruct((n_tiles * tr, d), F32),
        compiler_params=_params("arbitrary"),
        name="moe_experts",
    )(tile_e, src3, src3, x2, w1.astype(BF16), w3.astype(BF16), w2.astype(BF16))

    nf = COMBINE_FRAMES
    cb = tb // nf
    n_fb, n_cb = n_f // nf, n_c // cb
    n_tb = n_fb * n_cb
    tile_order = lambda a: a.reshape(n_fb, nf, n_cb, cb).transpose(0, 2, 1, 3).reshape(n_tb, 1, tb)
    d1, d2 = tile_order(dest1), tile_order(dest2)
    in_blk = lambda i: (i // n_cb, i % n_cb, 0)
    out = pl.pallas_call(
        functools.partial(_moe_combine_kernel, alpha=alpha),
        grid=(n_tb,),
        in_specs=[smem_blk(tb), smem_blk(tb), smem_next(tb, n_tb - 1), smem_next(tb, n_tb - 1),
                  pl.BlockSpec((nf, cb, d), in_blk), pl.BlockSpec((nf, cb, n_e), in_blk),
                  pl.BlockSpec(memory_space=pl.ANY), _const_spec((1, d)), _const_spec((1, d))],
        out_specs=pl.BlockSpec((cb, nf, d), lambda i: (i % n_cb, i // n_cb, 0)),
        out_shape=jax.ShapeDtypeStruct((n_c, n_f, d), F32),
        scratch_shapes=[pltpu.VMEM((2, 2, tb, d), F32), pltpu.SemaphoreType.DMA((2,))],
        compiler_params=_params("arbitrary"),
        name="moe_combine",
    )(d1, d2, d1, d2, xt, route_g.reshape(n_f, n_c, n_e), ys, ln_g.reshape(1, d), ln_b.reshape(1, d))
    return out.reshape(t, d)


def kernel(x, gdn_w_in, gdn_conv_w, gdn_a_log, gdn_dt_bias, gdn_norm_g, gdn_w_out, ffn_w1, ffn_w3, ffn_w2,
           s5_w_in, s5_lam_re, s5_lam_im, s5_log_dt, s5_b_re, s5_b_im, s5_c_re, s5_c_im, s5_d, s5_w_glu,
           moe_w_router, moe_b_router, moe_w1, moe_w3, moe_w2, ln_g, ln_b):
    bsz, seq, d = x.shape
    depth = ln_g.shape[0]
    alpha = (2 * depth) ** 0.25
    x2 = x.reshape(bsz * seq, d)
    for i in range(depth):
        j = i // 2
        if i % 2 == 0:
            x2 = _gdn_layer(x2, bsz, seq, gdn_w_in[j], gdn_conv_w[j], gdn_a_log[j], gdn_dt_bias[j], gdn_norm_g[j],
                            gdn_w_out[j], ln_g[i, 0], ln_b[i, 0], alpha)
            x2 = _ffn_ln(x2, ffn_w1[j], ffn_w3[j], ffn_w2[j], ln_g[i, 1], ln_b[i, 1], alpha,
                         chunk_major=i + 1 < depth)
        else:
            xt = _s5_layer(x2, seq, s5_w_in[j], s5_lam_re[j], s5_lam_im[j], s5_log_dt[j], s5_b_re[j], s5_b_im[j],
                           s5_c_re[j], s5_c_im[j], s5_d[j], s5_w_glu[j], ln_g[i, 0], ln_b[i, 0], alpha)
            x2 = _moe_layer(xt, moe_w_router[j], moe_b_router[j], moe_w1[j], moe_w3[j], moe_w2[j],
                            ln_g[i, 1], ln_b[i, 1], alpha)
    return x2.reshape(bsz, seq, d)
```

```python
import functools
import math

import jax
import jax.numpy as jnp
from jax import lax
from jax.experimental import pallas as pl
from jax.experimental.pallas import tpu as pltpu

F32 = jnp.float32
BF16 = jnp.bfloat16

CHUNK = 64
GDN_HEADS = 8
GDN_DK = 128
GDN_CONV = 4
S5_GROUP = 16
S5_STATE = 64
S5_TBLK = 16
N_EXPERTS = 8
COMBINE_FRAMES = 8
LN_EPS = 1e-5
NORM_EPS = 1e-6

VMEM_LIMIT_BYTES = 56 * 1024 * 1024
CONV_CARRY_ROWS = 8
MXU_COLS = 256
PREP_CHUNKS = 4


def _params(*semantics):
    return pltpu.CompilerParams(dimension_semantics=semantics, vmem_limit_bytes=VMEM_LIMIT_BYTES)


def _const_spec(shape):
    nd = len(shape)
    return pl.BlockSpec(shape, lambda *_: (0,) * nd, pipeline_mode=pl.Buffered(1))


def _dot(a, b):
    return jnp.dot(a.astype(BF16), b.astype(BF16), preferred_element_type=F32)


def _dot_nt(a, b):
    return lax.dot_general(a.astype(BF16), b.astype(BF16), (((1,), (1,)), ((), ())),
                           preferred_element_type=F32)


def _dot_tn(a, b):
    return lax.dot_general(a.astype(BF16), b.astype(BF16), (((0,), (0,)), ((), ())),
                           preferred_element_type=F32)


def _dot_f32(a, b):
    return jnp.dot(a, b, preferred_element_type=F32, precision=lax.Precision.HIGHEST)


def _silu(x):
    return x * jax.nn.sigmoid(x)


def _gelu(x):
    return 0.5 * x * (1.0 + lax.erf(x * (2.0 ** -0.5)))


def _softplus(x):
    return jnp.maximum(x, 0.0) + jnp.log1p(jnp.exp(-jnp.abs(x)))


def _layer_norm(y, g, b):
    mu = jnp.mean(y, axis=-1, keepdims=True)
    yc = y - mu
    var = jnp.mean(yc * yc, axis=-1, keepdims=True)
    return yc * lax.rsqrt(var + LN_EPS) * g + b


def _gdn_in_kernel(x_ref, wqkv_ref, wz_ref, wb_ref, wa_ref, waT_ref, conv_ref, alog_ref, dtb_ref,
                   alogc_ref, dtbc_ref, tri_ref, triT_ref,
                   q_ref, k_ref, v_ref, z_ref, beta_ref, gc_ref, gcT_ref, carry_ref,
                   *, tiles_per_seq, tm, n_qk):
    i = pl.program_id(0)

    @pl.when(i % tiles_per_seq == 0)
    def _():
        carry_ref[...] = jnp.zeros(carry_ref.shape, F32)

    xb = x_ref[...].astype(BF16)
    z_ref[...] = jnp.dot(xb, wz_ref[...], preferred_element_type=F32).astype(z_ref.dtype)
    beta_ref[...] = jax.nn.sigmoid(jnp.dot(xb, wb_ref[...], preferred_element_type=F32))

    g_col = -jnp.exp(alog_ref[...]) * _softplus(jnp.dot(xb, wa_ref[...], preferred_element_type=F32) + dtb_ref[...])
    for c in range(tm // CHUNK):
        rows = slice(c * CHUNK, (c + 1) * CHUNK)
        gc_ref[rows, :] = _dot_f32(tri_ref[...], g_col[rows, :])
        a_row = lax.dot_general(waT_ref[...], xb[rows, :], (((1,), (1,)), ((), ())), preferred_element_type=F32)
        g_row = -jnp.exp(alogc_ref[...]) * _softplus(a_row + dtbc_ref[...])
        gcT_ref[c] = _dot_f32(g_row, triT_ref[...])

    for blk in range(carry_ref.shape[1] // MXU_COLS):
        cols = slice(blk * MXU_COLS, (blk + 1) * MXU_COLS)
        p = jnp.dot(xb, wqkv_ref[:, cols], preferred_element_type=F32)
        ext = jnp.concatenate([carry_ref[:, cols], p], axis=0)
        carry_ref[:, cols] = p[tm - CONV_CARRY_ROWS:tm, :]
        acc = conv_ref[GDN_CONV - 1:GDN_CONV, cols] * p
        for j in range(1, GDN_CONV):
            shifted = pltpu.roll(ext, j, axis=0)[CONV_CARRY_ROWS:, :]
            acc = acc + conv_ref[GDN_CONV - 1 - j:GDN_CONV - j, cols] * shifted
        y = _silu(acc)
        for hh in range(MXU_COLS // GDN_DK):
            head = blk * (MXU_COLS // GDN_DK) + hh
            yh = y[:, hh * GDN_DK:(hh + 1) * GDN_DK]
            if head < 2 * n_qk:
                yh = yh * lax.rsqrt(jnp.sum(yh * yh, axis=-1, keepdims=True) + NORM_EPS)
            if head < n_qk:
                q_ref[:, head * GDN_DK:(head + 1) * GDN_DK] = (yh * (GDN_DK ** -0.5)).astype(q_ref.dtype)
            elif head < 2 * n_qk:
                k_ref[:, (head - n_qk) * GDN_DK:(head - n_qk + 1) * GDN_DK] = yh.astype(k_ref.dtype)
            else:
                v_ref[:, (head - 2 * n_qk) * GDN_DK:(head - 2 * n_qk + 1) * GDN_DK] = yh.astype(v_ref.dtype)


def _gdn_in(x2, w_in, conv_w, a_log, dt_bias, seq, tm=256):
    t, d = x2.shape
    tm = min(tm, seq)
    nh = GDN_HEADS
    qk = nh * GDN_DK
    dv = (w_in.shape[1] - 2 * qk - 2 * nh) // 2
    wqkv = w_in[:, :2 * qk + dv].astype(BF16)
    wz = w_in[:, 2 * qk + dv:2 * qk + 2 * dv].astype(BF16)
    wb = w_in[:, 2 * qk + 2 * dv:2 * qk + 2 * dv + nh].astype(BF16)
    wa = w_in[:, 2 * qk + 2 * dv + nh:].astype(BF16)
    idx = jnp.arange(CHUNK)
    tri = (idx[:, None] >= idx[None, :]).astype(F32)
    n_chunks = t // CHUNK
    kern = functools.partial(_gdn_in_kernel, tiles_per_seq=seq // tm, tm=tm, n_qk=nh)
    row = lambda i: (i, 0)
    return pl.pallas_call(
        kern,
        grid=(t // tm,),
        in_specs=[
            pl.BlockSpec((tm, d), row),
            _const_spec(wqkv.shape), _const_spec(wz.shape), _const_spec(wb.shape), _const_spec(wa.shape),
            _const_spec((nh, d)), _const_spec(conv_w.shape),
            _const_spec((1, nh)), _const_spec((1, nh)), _const_spec((nh, 1)), _const_spec((nh, 1)),
            _const_spec((CHUNK, CHUNK)), _const_spec((CHUNK, CHUNK)),
        ],
        out_specs=[
            pl.BlockSpec((tm, qk), row), pl.BlockSpec((tm, qk), row), pl.BlockSpec((tm, dv), row),
            pl.BlockSpec((tm, dv), row), pl.BlockSpec((tm, nh), row), pl.BlockSpec((tm, nh), row),
            pl.BlockSpec((tm // CHUNK, nh, CHUNK), lambda i: (i, 0, 0)),
        ],
        out_shape=[
            jax.ShapeDtypeStruct((t, qk), BF16), jax.ShapeDtypeStruct((t, qk), BF16),
            jax.ShapeDtypeStruct((t, dv), BF16), jax.ShapeDtypeStruct((t, dv), BF16),
            jax.ShapeDtypeStruct((t, nh), F32), jax.ShapeDtypeStruct((t, nh), F32),
            jax.ShapeDtypeStruct((n_chunks, nh, CHUNK), F32),
        ],
        scratch_shapes=[pltpu.VMEM((CONV_CARRY_ROWS, 2 * qk + dv), F32)],
        compiler_params=_params("arbitrary"),
        name="gdn_in",
    )(x2, wqkv, wz, wb, wa, wa.T, conv_w.astype(F32),
      a_log.reshape(1, nh).astype(F32), dt_bias.reshape(1, nh).astype(F32),
      a_log.reshape(nh, 1).astype(F32), dt_bias.reshape(nh, 1).astype(F32), tri, tri.T)


def _gdn_core_kernel(q_ref, k_ref, v_ref, z_ref, beta_ref, gc_ref, gcT_ref, ng_ref, o_ref,
                     s_ref, u_ref, wq_ref, kd_ref, a_ref, *, n_sub):
    bsz, n_pair = s_ref.shape[0], s_ref.shape[1]
    pw = 2 * GDN_DK
    cc = CHUNK

    @pl.when(pl.program_id(0) == 0)
    def _():
        s_ref[...] = jnp.zeros(s_ref.shape, F32)

    r = lax.broadcasted_iota(jnp.int32, (cc, 2 * cc), 0)
    lane_m = lax.broadcasted_iota(jnp.int32, (cc, 2 * cc), 1)
    c = lane_m & (cc - 1)
    causal = r >= c
    strict = r > c
    eye = jnp.where(r == c, 1.0, 0.0)
    first_m = lane_m < cc
    first_w = lax.broadcasted_iota(jnp.int32, (cc, pw), 1) < GDN_DK
    sr = lax.broadcasted_iota(jnp.int32, (pw, pw), 0) < GDN_DK
    sc = lax.broadcasted_iota(jnp.int32, (pw, pw), 1) < GDN_DK
    same_head = sr == sc
    first_row = lax.broadcasted_iota(jnp.int32, (pw, 1), 0) < GDN_DK

    def block_diag(x, first):
        return jnp.concatenate([jnp.where(first, x, 0.0), jnp.where(first, 0.0, x)], axis=0)

    def pair_cols(col8, p, first):
        return jnp.where(first, col8[:, 2 * p:2 * p + 1], col8[:, 2 * p + 1:2 * p + 2])

    def prep(it, carry):
        chains = []
        for j in range(PREP_CHUNKS):
            idx = it * PREP_CHUNKS + j
            b, ci = idx // n_sub, idx % n_sub
            rows = pl.ds(pl.multiple_of(ci * cc, cc), cc)
            beta, gc, gct = beta_ref[b, rows, :], gc_ref[b, rows, :], gcT_ref[b, ci]
            for p in range(n_pair):
                chains.append((b, ci, rows, p, beta, gc, gct))
        ch = range(len(chains))
        lanes = [slice(p * pw, (p + 1) * pw) for (_, _, _, p, _, _, _) in chains]
        q = [q_ref[b, rows, lanes[i]].astype(F32) for i, (b, _, rows, *_) in enumerate(chains)]
        k = [k_ref[b, rows, lanes[i]].astype(F32) for i, (b, _, rows, *_) in enumerate(chains)]
        v = [v_ref[b, rows, lanes[i]].astype(F32) for i, (b, _, rows, *_) in enumerate(chains)]
        g_m = [pair_cols(gc, p, first_m) for (_, _, _, p, _, gc, _) in chains]
        g_w = [pair_cols(gc, p, first_w) for (_, _, _, p, _, gc, _) in chains]
        b_w = [pair_cols(beta, p, first_w) for (_, _, _, p, beta, _, _) in chains]
        decay = [jnp.exp(jnp.where(causal, g_m[i] - chains[i][6][chains[i][3]:chains[i][3] + 1, :], -jnp.inf)) for i in ch]
        eg = [jnp.exp(g_w[i]) for i in ch]
        kb = [k[i] * b_w[i] for i in ch]
        k_bd = [block_diag(k[i], first_w) for i in ch]
        m = [jnp.where(strict, _dot_nt(kb[i], k_bd[i]) * decay[i], 0.0) for i in ch]
        qk = [_dot_nt(q[i], k_bd[i]) for i in ch]
        t_inv = [eye - m[i] for i in ch]
        step = 1
        while 2 * step <= cc // 2:
            m = [_dot(m[i], block_diag(m[i], first_m)) for i in ch]
            t_inv = [t_inv[i] + _dot(t_inv[i], block_diag(m[i], first_m)) for i in ch]
            step *= 2
        rhs = [jnp.concatenate([block_diag(v[i] * b_w[i], first_w), block_diag(kb[i] * eg[i], first_w)], axis=1)
               for i in ch]
        uw = [_dot(t_inv[i], rhs[i]) for i in ch]
        for i, (b, ci, _, p, _, _, _) in enumerate(chains):
            u_ref[b, ci, p] = uw[i][:, :pw]
            wq_ref[b, ci, p, 0:cc, :] = uw[i][:, pw:].astype(BF16)
            wq_ref[b, ci, p, cc:2 * cc, :] = (q[i] * eg[i]).astype(BF16)
            kd_ref[b, ci, p] = (k[i] * jnp.exp(g_w[i][cc - 1:cc, :] - g_w[i])).T.astype(BF16)
            a_ref[b, ci, p] = (qk[i] * decay[i]).astype(BF16)
        return carry

    lax.fori_loop(0, bsz * n_sub // PREP_CHUNKS, prep, 0)

    chains = [(b, p) for b in range(bsz) for p in range(n_pair)]
    ch = range(len(chains))

    def scan(ci, carry):
        rows = pl.ds(pl.multiple_of(ci * cc, cc), cc)
        e_last = [jnp.exp(gc_ref[b, pl.ds(ci * cc + cc - 1, 1), :]) for b in range(bsz)]
        s = [s_ref[b, p] for (b, p) in chains]
        ws = [jnp.dot(wq_ref[b, ci, p], s[i].astype(BF16), preferred_element_type=F32) for i, (b, p) in enumerate(chains)]
        v_new = [u_ref[b, ci, p] - ws[i][:cc] for i, (b, p) in enumerate(chains)]
        o = [ws[i][cc:] + jnp.dot(a_ref[b, ci, p], block_diag(v_new[i], first_w).astype(BF16),
                                  preferred_element_type=F32) for i, (b, p) in enumerate(chains)]
        upd = [_dot(kd_ref[b, ci, p], v_new[i]) for i, (b, p) in enumerate(chains)]
        for i, (b, p) in enumerate(chains):
            e_col = jnp.where(first_row, e_last[b][:, 2 * p:2 * p + 1], e_last[b][:, 2 * p + 1:2 * p + 2])
            s_ref[b, p] = s[i] * e_col + jnp.where(same_head, upd[i], 0.0)
            for hh in range(2):
                cols = slice((2 * p + hh) * GDN_DK, (2 * p + hh + 1) * GDN_DK)
                oh = o[i][:, hh * GDN_DK:(hh + 1) * GDN_DK]
                on = oh * lax.rsqrt(jnp.mean(oh * oh, axis=-1, keepdims=True) + NORM_EPS) * ng_ref[...]
                o_ref[b, rows, cols] = (on * _silu(z_ref[b, rows, cols].astype(F32))).astype(o_ref.dtype)
        return carry

    lax.fori_loop(0, n_sub, scan, 0)


def _gdn_core(q, k, v, z, beta, gc, gct, norm_g, bsz, seq, tc=256):
    t, qk = q.shape
    tc = min(tc, seq)
    nh = GDN_HEADS
    dv = v.shape[1]
    n_sub = tc // CHUNK
    n_pair = nh // 2
    pw = 2 * GDN_DK
    blk3 = lambda w: pl.BlockSpec((bsz, tc, w), lambda i: (0, i, 0))
    seq3 = lambda a: a.reshape(bsz, seq, a.shape[-1])
    o = pl.pallas_call(
        functools.partial(_gdn_core_kernel, n_sub=n_sub),
        grid=(seq // tc,),
        in_specs=[blk3(qk), blk3(qk), blk3(dv), blk3(dv), blk3(nh), blk3(nh),
                  pl.BlockSpec((bsz, n_sub, n_pair, 2 * CHUNK), lambda i: (0, i, 0, 0)),
                  _const_spec((1, dv // nh))],
        out_specs=blk3(dv),
        out_shape=jax.ShapeDtypeStruct((bsz, seq, dv), BF16),
        scratch_shapes=[pltpu.VMEM((bsz, n_pair, pw, pw), F32),
                        pltpu.VMEM((bsz, n_sub, n_pair, CHUNK, pw), F32),
                        pltpu.VMEM((bsz, n_sub, n_pair, 2 * CHUNK, pw), BF16),
                        pltpu.VMEM((bsz, n_sub, n_pair, pw, CHUNK), BF16),
                        pltpu.VMEM((bsz, n_sub, n_pair, CHUNK, 2 * CHUNK), BF16)],
        compiler_params=_params("arbitrary"),
        name="gdn_core",
    )(seq3(q), seq3(k), seq3(v), seq3(z), seq3(beta), seq3(gc),
      gct.reshape(bsz, seq // CHUNK, n_pair, 2 * CHUNK), norm_g.reshape(1, -1).astype(F32))
    return o.reshape(t, dv)


def _proj_ln_kernel(a_ref, w_ref, x_ref, g_ref, b_ref, o_ref, *, alpha):
    h = jnp.dot(a_ref[...].astype(BF16), w_ref[...], preferred_element_type=F32)
    o_ref[...] = _layer_norm(alpha * x_ref[...] + h, g_ref[...], b_ref[...])


def _proj_ln(a, w, x2, ln_g, ln_b, alpha, tm=512):
    t, d = x2.shape
    tm = min(tm, t)
    row = lambda i: (i, 0)
    return pl.pallas_call(
        functools.partial(_proj_ln_kernel, alpha=alpha),
        grid=(t // tm,),
        in_specs=[pl.BlockSpec((tm, a.shape[1]), row), _const_spec(w.shape), pl.BlockSpec((tm, d), row),
                  _const_spec((1, d)), _const_spec((1, d))],
        out_specs=pl.BlockSpec((tm, d), row),
        out_shape=jax.ShapeDtypeStruct((t, d), F32),
        compiler_params=_params("arbitrary"),
        name="proj_ln",
    )(a, w.astype(BF16), x2, ln_g.reshape(1, d), ln_b.reshape(1, d))


def _ffn_ln_kernel(x_ref, w1_ref, w3_ref, w2_ref, g_ref, b_ref, o_ref, *, alpha, chunk_major):
    x = x_ref[...]
    xb = x.astype(BF16)
    hid = _silu(jnp.dot(xb, w1_ref[...], preferred_element_type=F32)) * jnp.dot(xb, w3_ref[...], preferred_element_type=F32)
    f = jnp.dot(hid.astype(BF16), w2_ref[...], preferred_element_type=F32)
    y = _layer_norm(alpha * x + f, g_ref[...], b_ref[...])
    if chunk_major:
        for c in range(o_ref.shape[1]):
            o_ref[:, c, :] = y[c * CHUNK:(c + 1) * CHUNK, :]
    else:
        o_ref[...] = y


def _ffn_ln(x2, w1, w3, w2, ln_g, ln_b, alpha, chunk_major=False, tm=512):
    t, d = x2.shape
    tm = min(tm, t)
    row = lambda i: (i, 0)
    if chunk_major:
        out_spec = pl.BlockSpec((CHUNK, tm // CHUNK, d), lambda i: (0, i, 0))
        out_shape = jax.ShapeDtypeStruct((CHUNK, t // CHUNK, d), F32)
    else:
        out_spec = pl.BlockSpec((tm, d), row)
        out_shape = jax.ShapeDtypeStruct((t, d), F32)
    return pl.pallas_call(
        functools.partial(_ffn_ln_kernel, alpha=alpha, chunk_major=chunk_major),
        grid=(t // tm,),
        in_specs=[pl.BlockSpec((tm, d), row), _const_spec(w1.shape), _const_spec(w3.shape), _const_spec(w2.shape),
                  _const_spec((1, d)), _const_spec((1, d))],
        out_specs=out_spec,
        out_shape=out_shape,
        compiler_params=_params("arbitrary"),
        name="ffn_ln",
    )(x2, w1.astype(BF16), w3.astype(BF16), w2.astype(BF16), ln_g.reshape(1, d), ln_b.reshape(1, d))


def _gdn_layer(x2, bsz, seq, w_in, conv_w, a_log, dt_bias, norm_g, w_out, ln_g, ln_b, alpha):
    q, k, v, z, beta, gc, gct = _gdn_in(x2, w_in, conv_w, a_log, dt_bias, seq)
    o = _gdn_core(q, k, v, z, beta, gc, gct, norm_g, bsz, seq)
    return _proj_ln(o, w_out, x2, ln_g, ln_b, alpha)


def _s5_taps_kernel(ce_ref, b_ref, o_ref):
    o_ref[0] = _dot_f32(ce_ref[0], b_ref[0])


def _s5_tables(lam_re, lam_im, log_dt, b_re, b_im, c_re, c_im, d_skip, n_per_seq):
    g, p = lam_re.shape
    gs = b_re.shape[-1]
    lr, li = lam_re.astype(F32), lam_im.astype(F32)
    dt = jnp.exp(log_dt.astype(F32))[:, None]
    zr, zi = lr * dt, li * dt

    def a_pow(steps):
        s = steps.astype(F32)[:, None, None]
        mag = jnp.exp(zr[None] * s)
        return mag * jnp.cos(zi[None] * s), mag * jnp.sin(zi[None] * s)

    ar, ai = a_pow(jnp.arange(CHUNK + 1))
    nr, ni = ar[1] - 1.0, ai[1]
    den = lr * lr + li * li
    qr, qi = (nr * lr + ni * li) / den, (ni * lr - nr * li) / den
    bbr = qr[..., None] * b_re - qi[..., None] * b_im
    bbi = qr[..., None] * b_im + qi[..., None] * b_re
    cr, ci = c_re.astype(F32), c_im.astype(F32)
    car = cr[:, None] * jnp.moveaxis(ar, 0, 1)[:, :, None, :] - ci[:, None] * jnp.moveaxis(ai, 0, 1)[:, :, None, :]
    cai = cr[:, None] * jnp.moveaxis(ai, 0, 1)[:, :, None, :] + ci[:, None] * jnp.moveaxis(ar, 0, 1)[:, :, None, :]
    ca = jnp.concatenate([car, -cai], axis=-1)
    ce = ca[:, :CHUNK].reshape(g, CHUNK * gs, 2 * p)
    bst = jnp.concatenate([bbr, bbi], axis=1)
    taps = pl.pallas_call(
        _s5_taps_kernel,
        grid=(g,),
        in_specs=[pl.BlockSpec((1, CHUNK * gs, 2 * p), lambda i: (i, 0, 0)),
                  pl.BlockSpec((1, 2 * p, gs), lambda i: (i, 0, 0))],
        out_specs=pl.BlockSpec((1, CHUNK * gs, gs), lambda i: (i, 0, 0)),
        out_shape=jax.ShapeDtypeStruct((g, CHUNK * gs, gs), F32),
        compiler_params=_params("arbitrary"),
        name="s5_taps",
    )(ce, bst).reshape(g, CHUNK, gs, gs)
    taps = taps.at[:, 0].add(d_skip.astype(F32).reshape(g, gs)[:, :, None] * jnp.eye(gs, dtype=F32))
    nb = CHUNK // S5_TBLK
    blk = jnp.arange(S5_TBLK)
    lag = S5_TBLK * jnp.arange(nb)[:, None, None] + blk[None, :, None] - blk[None, None, :]
    stk = jnp.where((lag >= 0)[None, :, :, :, None, None], taps[:, jnp.clip(lag, 0, CHUNK - 1)], 0.0)
    m_mat = jnp.transpose(stk, (0, 1, 2, 4, 3, 5)).reshape(g, CHUNK * gs, S5_TBLK * gs).astype(BF16)
    arr, aii = jnp.moveaxis(ar[:CHUNK][::-1], 0, 1), jnp.moveaxis(ai[:CHUNK][::-1], 0, 1)
    wr = arr[:, :, :, None] * bbr[:, None] - aii[:, :, :, None] * bbi[:, None]
    wi = arr[:, :, :, None] * bbi[:, None] + aii[:, :, :, None] * bbr[:, None]
    w_mat = jnp.concatenate([jnp.transpose(wr, (0, 2, 1, 3)), jnp.transpose(wi, (0, 2, 1, 3))], axis=1)
    w_mat = w_mat.reshape(g, 2 * p, CHUNK * gs).astype(BF16)
    v_mat = ca[:, 1:].reshape(g, CHUNK * gs, 2 * p).astype(BF16)
    n_steps = max(1, (n_per_seq - 1).bit_length())
    apr, api = a_pow(CHUNK * 2 ** jnp.arange(n_steps))
    apr = jnp.moveaxis(apr, 0, 1)[..., None]
    api = jnp.moveaxis(api, 0, 1)[..., None]
    return m_mat, w_mat, v_mat, apr, api


def _s5_in_kernel(x_ref, wT_ref, o_ref):
    o_ref[0] = _dot_nt(wT_ref[...], x_ref[0]).astype(o_ref.dtype)


def _s5_scan_kernel(u_ref, m_ref, w_ref, v_ref, apr_ref, api_ref, o_ref, *, n_per_seq):
    rows, gs, cols = u_ref.shape
    p = apr_ref.shape[2]
    u = u_ref[...].reshape(rows * gs, cols)
    hin = jnp.dot(w_ref[0], u, preferred_element_type=F32)
    yr, yi = hin[:p], hin[p:]
    pos = lax.broadcasted_iota(jnp.int32, (p, cols), 1) % n_per_seq

    def shifted(a, d):
        return jnp.where(pos >= d, pltpu.roll(a, d, axis=1), 0.0)

    d, kk = 1, 0
    while d < n_per_seq:
        sr, si = shifted(yr, d), shifted(yi, d)
        ar, ai = apr_ref[0, kk], api_ref[0, kk]
        yr, yi = yr + ar * sr - ai * si, yi + ar * si + ai * sr
        d, kk = 2 * d, kk + 1
    h_prev = jnp.concatenate([shifted(yr, 1), shifted(yi, 1)], axis=0).astype(BF16)
    nb = rows // S5_TBLK
    br = S5_TBLK * gs
    ys = [jnp.dot(v_ref[0, b * br:(b + 1) * br, :], h_prev, preferred_element_type=F32) for b in range(nb)]
    for sb in range(nb):
        part = jnp.dot(m_ref[0, 0:(nb - sb) * br, :], u[sb * br:(sb + 1) * br, :], preferred_element_type=F32)
        for j in range(nb - sb):
            ys[sb + j] = ys[sb + j] + part[j * br:(j + 1) * br, :]
    y = jnp.concatenate(ys, axis=0)
    o_ref[0] = _gelu(y).reshape(rows, gs, cols).astype(o_ref.dtype)


def _s5_out_kernel(h_ref, w_ref, x_ref, g_ref, b_ref, o_ref, *, alpha):
    d = x_ref.shape[-1]
    ht = h_ref[...].reshape(d, h_ref.shape[-1])
    vg = lax.dot_general(ht, w_ref[...], (((0,), (0,)), ((), ())), preferred_element_type=F32)
    o_ref[0] = _layer_norm(alpha * x_ref[0] + vg[:, :d] * jax.nn.sigmoid(vg[:, d:]), g_ref[...], b_ref[...])


def _s5_layer(xt, seq, w_in, lam_re, lam_im, log_dt, b_re, b_im, c_re, c_im, d_skip, w_glu, ln_g, ln_b, alpha):
    _, cols, d = xt.shape
    n_per_seq = seq // CHUNK
    g, p = lam_re.shape
    gs = d // g
    m_mat, w_mat, v_mat, apr, api = _s5_tables(lam_re, lam_im, log_dt, b_re, b_im, c_re, c_im, d_skip, n_per_seq)
    u_t = pl.pallas_call(
        _s5_in_kernel,
        grid=(CHUNK,),
        in_specs=[pl.BlockSpec((1, cols, d), lambda i: (i, 0, 0)), _const_spec((d, d))],
        out_specs=pl.BlockSpec((1, d, cols), lambda i: (i, 0, 0)),
        out_shape=jax.ShapeDtypeStruct((CHUNK, d, cols), BF16),
        compiler_params=_params("arbitrary"),
        name="s5_in",
    )(xt, w_in.T.astype(BF16))
    n_steps = apr.shape[1]
    hid = pl.pallas_call(
        functools.partial(_s5_scan_kernel, n_per_seq=n_per_seq),
        grid=(g,),
        in_specs=[pl.BlockSpec((CHUNK, gs, cols), lambda i: (0, i, 0)),
                  pl.BlockSpec((1, CHUNK * gs, S5_TBLK * gs), lambda i: (i, 0, 0)),
                  pl.BlockSpec((1, 2 * p, CHUNK * gs), lambda i: (i, 0, 0)),
                  pl.BlockSpec((1, CHUNK * gs, 2 * p), lambda i: (i, 0, 0)),
                  pl.BlockSpec((1, n_steps, p, 1), lambda i: (i, 0, 0, 0)),
                  pl.BlockSpec((1, n_steps, p, 1), lambda i: (i, 0, 0, 0))],
        out_specs=pl.BlockSpec((1, CHUNK, gs, cols), lambda i: (i, 0, 0, 0)),
        out_shape=jax.ShapeDtypeStruct((g, CHUNK, gs, cols), BF16),
        compiler_params=_params("arbitrary"),
        name="s5_scan",
    )(u_t, m_mat, w_mat, v_mat, apr, api)
    return pl.pallas_call(
        functools.partial(_s5_out_kernel, alpha=alpha),
        grid=(CHUNK,),
        in_specs=[pl.BlockSpec((g, 1, gs, cols), lambda i: (0, i, 0, 0)), _const_spec(w_glu.shape),
                  pl.BlockSpec((1, cols, d), lambda i: (i, 0, 0)), _const_spec((1, d)), _const_spec((1, d))],
        out_specs=pl.BlockSpec((1, cols, d), lambda i: (i, 0, 0)),
        out_shape=jax.ShapeDtypeStruct((CHUNK, cols, d), F32),
        compiler_params=_params("arbitrary"),
        name="s5_out",
    )(hid, w_glu.astype(BF16), xt, ln_g.reshape(1, d), ln_b.reshape(1, d))


def _top2_gates(logits):
    n_e = logits.shape[-1]
    lane = lax.broadcasted_iota(jnp.int32, logits.shape, 1)
    m1 = jnp.max(logits, axis=-1, keepdims=True)
    i1 = jnp.min(jnp.where(logits == m1, lane, n_e), axis=-1, keepdims=True)
    rest = jnp.where(lane == i1, -jnp.inf, logits)
    m2 = jnp.max(rest, axis=-1, keepdims=True)
    i2 = jnp.min(jnp.where(rest == m2, lane, n_e), axis=-1, keepdims=True)
    e2 = jnp.exp(m2 - m1)
    den = 1.0 + e2
    return i1, i2, 1.0 / den, e2 / den


def _moe_route_kernel(x_ref, wr_ref, br_ref, low_ref, idx_ref, gate_ref, cnt_ref, base_ref):
    @pl.when(pl.program_id(0) == 0)
    def _():
        base_ref[...] = jnp.zeros(base_ref.shape, F32)

    i1, i2, g1, g2 = _top2_gates(_dot_f32(x_ref[...], wr_ref[...]) + br_ref[...])
    lane = lax.broadcasted_iota(jnp.int32, (x_ref.shape[0], wr_ref.shape[1]), 1)
    sel = jnp.where((lane == i1) | (lane == i2), 1.0, 0.0)
    rank = base_ref[...] + jnp.dot(low_ref[...], sel.astype(BF16), preferred_element_type=F32)
    p1 = jnp.sum(jnp.where(lane == i1, rank, 0.0), axis=-1, keepdims=True).astype(jnp.int32)
    p2 = jnp.sum(jnp.where(lane == i2, rank, 0.0), axis=-1, keepdims=True).astype(jnp.int32)
    idx_ref[...] = jnp.where(lane == 0, i1, jnp.where(lane == 1, i2, jnp.where(lane == 2, p1, jnp.where(lane == 3, p2, 0))))
    gate_ref[...] = jnp.where(lane == 0, g1, jnp.where(lane == 1, g2, 0.0))
    base_ref[...] += jnp.sum(sel, axis=0, keepdims=True)
    cnt_ref[...] = base_ref[...].astype(jnp.int32)


def _row_gather_start(src_hbm, idx_ref, dst_ref, sem, n_rows):
    for r in range(n_rows):
        pltpu.make_async_copy(src_hbm.at[pl.ds(idx_ref[0, 0, r], 1)], dst_ref.at[pl.ds(r, 1)], sem).start()


def _row_gather_wait(src_hbm, dst_ref, sem, n_rows):
    pltpu.make_async_copy(src_hbm.at[pl.ds(0, n_rows)], dst_ref, sem).wait()


def _moe_expert_kernel(tile_e_ref, src_ref, src_next_ref, x_hbm, w1_ref, w3_ref, w2_ref, y_ref, xbuf, sem):
    del tile_e_ref
    i = pl.program_id(0)
    n = pl.num_programs(0)
    tr = xbuf.shape[1]
    slot = i % 2

    @pl.when(i == 0)
    def _():
        _row_gather_start(x_hbm, src_ref, xbuf.at[0], sem.at[0], tr)

    @pl.when(i + 1 < n)
    def _():
        _row_gather_start(x_hbm, src_next_ref, xbuf.at[1 - slot], sem.at[1 - slot], tr)

    _row_gather_wait(x_hbm, xbuf.at[slot], sem.at[slot], tr)
    xb = xbuf[slot].astype(BF16)
    hid = _silu(jnp.dot(xb, w1_ref[0], preferred_element_type=F32)) * jnp.dot(xb, w3_ref[0], preferred_element_type=F32)
    y_ref[...] = jnp.dot(hid.astype(BF16), w2_ref[0], preferred_element_type=F32)


def _moe_combine_kernel(d1_ref, d2_ref, d1n_ref, d2n_ref, x_ref, gate_ref, y_hbm, g_ref, b_ref, o_ref, ybuf, sem,
                        *, alpha):
    i = pl.program_id(0)
    n = pl.num_programs(0)
    nf, cb, d = x_ref.shape
    tb = nf * cb
    slot = i % 2

    def start(a_ref, b_ref, s):
        _row_gather_start(y_hbm, a_ref, ybuf.at[s, 0], sem.at[s], tb)
        _row_gather_start(y_hbm, b_ref, ybuf.at[s, 1], sem.at[s], tb)

    @pl.when(i == 0)
    def _():
        start(d1_ref, d2_ref, 0)

    @pl.when(i + 1 < n)
    def _():
        start(d1n_ref, d2n_ref, 1 - slot)

    _row_gather_wait(y_hbm, ybuf.at[slot, 0], sem.at[slot], tb)
    _row_gather_wait(y_hbm, ybuf.at[slot, 1], sem.at[slot], tb)
    gate = gate_ref[...].reshape(tb, gate_ref.shape[-1])
    f = gate[:, 0:1] * ybuf[slot, 0] + gate[:, 1:2] * ybuf[slot, 1]
    y = _layer_norm(alpha * x_ref[...].reshape(tb, d) + f, g_ref[...], b_ref[...])
    for k in range(nf):
        o_ref[:, k, :] = y[k * cb:(k + 1) * cb, :]


def _moe_layer(xt, w_router, b_router, w1, w3, w2, ln_g, ln_b, alpha, tm=1024, tr=256, tb=256):
    n_f, n_c, d = xt.shape
    t = n_f * n_c
    x2 = xt.reshape(t, d)
    n_e, _, dff = w1.shape
    tm = min(tm, t)
    tb = min(tb, COMBINE_FRAMES * n_c)
    idx = jnp.arange(tm)
    low = (idx[:, None] > idx[None, :]).astype(BF16)
    route_i, route_g, counts = pl.pallas_call(
        _moe_route_kernel,
        grid=(t // tm,),
        in_specs=[pl.BlockSpec((tm, d), lambda i: (i, 0)), _const_spec((d, n_e)), _const_spec((1, n_e)),
                  _const_spec((tm, tm))],
        out_specs=[pl.BlockSpec((tm, n_e), lambda i: (i, 0)), pl.BlockSpec((tm, n_e), lambda i: (i, 0)),
                   pl.BlockSpec((1, n_e), lambda i: (0, 0))],
        out_shape=[jax.ShapeDtypeStruct((t, n_e), jnp.int32), jax.ShapeDtypeStruct((t, n_e), F32),
                   jax.ShapeDtypeStruct((1, n_e), jnp.int32)],
        scratch_shapes=[pltpu.VMEM((1, n_e), F32)],
        compiler_params=_params("arbitrary"),
        name="moe_route",
    )(x2, w_router.astype(F32), b_router.reshape(1, n_e).astype(F32), low)

    n_tiles = (2 * t) // tr + n_e
    cnt = counts[0]
    size = ((cnt + tr - 1) // tr) * tr
    ends = jnp.cumsum(size)
    offs = ends - size
    e1, e2, p1, p2 = route_i[:, 0], route_i[:, 1], route_i[:, 2], route_i[:, 3]
    dest1 = offs[e1] + p1
    dest2 = offs[e2] + p2
    tok = jnp.arange(t, dtype=jnp.int32)
    src = jnp.zeros((n_tiles * tr,), jnp.int32).at[jnp.concatenate([dest1, dest2])].set(jnp.concatenate([tok, tok]))
    tile_e = jnp.minimum(jnp.sum(jnp.arange(n_tiles, dtype=jnp.int32)[:, None] * tr >= ends[None, :], axis=1),
                         n_e - 1).astype(jnp.int32)
    src3 = src.reshape(n_tiles, 1, tr)

    smem_blk = lambda n: pl.BlockSpec((1, 1, n), lambda i, *_: (i, 0, 0), memory_space=pltpu.SMEM)
    smem_next = lambda n, last: pl.BlockSpec((1, 1, n), lambda i, *_: (jnp.minimum(i + 1, last), 0, 0),
                                             memory_space=pltpu.SMEM)
    wspec = lambda shape: pl.BlockSpec(shape, lambda i, te: (te[i], 0, 0))
    ys = pl.pallas_call(
        _moe_expert_kernel,
        grid_spec=pltpu.PrefetchScalarGridSpec(
            num_scalar_prefetch=1,
            grid=(n_tiles,),
            in_specs=[smem_blk(tr), smem_next(tr, n_tiles - 1), pl.BlockSpec(memory_space=pl.ANY),
                      wspec((1, d, dff)), wspec((1, d, dff)), wspec((1, dff, d))],
            out_specs=pl.BlockSpec((tr, d), lambda i, te: (i, 0)),
            scratch_shapes=[pltpu.VMEM((2, tr, d), F32), pltpu.SemaphoreType.DMA((2,))],
        ),
        out_shape=jax.ShapeDtypeStruct((n_tiles * tr, d), F32),
        compiler_params=_params("arbitrary"),
        name="moe_experts",
    )(tile_e, src3, src3, x2, w1.astype(BF16), w3.astype(BF16), w2.astype(BF16))

    nf = COMBINE_FRAMES
    cb = tb // nf
    n_fb, n_cb = n_f // nf, n_c // cb
    n_tb = n_fb * n_cb
    tile_order = lambda a: a.reshape(n_fb, nf, n_cb, cb).transpose(0, 2, 1, 3).reshape(n_tb, 1, tb)
    d1, d2 = tile_order(dest1), tile_order(dest2)
    in_blk = lambda i: (i // n_cb, i % n_cb, 0)
    out = pl.pallas_call(
        functools.partial(_moe_combine_kernel, alpha=alpha),
        grid=(n_tb,),
        in_specs=[smem_blk(tb), smem_blk(tb), smem_next(tb, n_tb - 1), smem_next(tb, n_tb - 1),
                  pl.BlockSpec((nf, cb, d), in_blk), pl.BlockSpec((nf, cb, n_e), in_blk),
                  pl.BlockSpec(memory_space=pl.ANY), _const_spec((1, d)), _const_spec((1, d))],
        out_specs=pl.BlockSpec((cb, nf, d), lambda i: (i % n_cb, i // n_cb, 0)),
        out_shape=jax.ShapeDtypeStruct((n_c, n_f, d), F32),
        scratch_shapes=[pltpu.VMEM((2, 2, tb, d), F32), pltpu.SemaphoreType.DMA((2,))],
        compiler_params=_params("arbitrary"),
        name="moe_combine",
    )(d1, d2, d1, d2, xt, route_g.reshape(n_f, n_c, n_e), ys, ln_g.reshape(1, d), ln_b.reshape(1, d))
    return out.reshape(t, d)


def kernel(x, gdn_w_in, gdn_conv_w, gdn_a_log, gdn_dt_bias, gdn_norm_g, gdn_w_out, ffn_w1, ffn_w3, ffn_w2,
           s5_w_in, s5_lam_re, s5_lam_im, s5_log_dt, s5_b_re, s5_b_im, s5_c_re, s5_c_im, s5_d, s5_w_glu,
           moe_w_router, moe_b_router, moe_w1, moe_w3, moe_w2, ln_g, ln_b):
    bsz, seq, d = x.shape
    depth = ln_g.shape[0]
    alpha = (2 * depth) ** 0.25
    x2 = x.reshape(bsz * seq, d)
    for i in range(depth):
        j = i // 2
        if i % 2 == 0:
            x2 = _gdn_layer(x2, bsz, seq, gdn_w_in[j], gdn_conv_w[j], gdn_a_log[j], gdn_dt_bias[j], gdn_norm_g[j],
                            gdn_w_out[j], ln_g[i, 0], ln_b[i, 0], alpha)
            x2 = _ffn_ln(x2, ffn_w1[j], ffn_w3[j], ffn_w2[j], ln_g[i, 1], ln_b[i, 1], alpha,
                         chunk_major=i + 1 < depth)
        else:
            xt = _s5_layer(x2, seq, s5_w_in[j], s5_lam_re[j], s5_lam_im[j], s5_log_dt[j], s5_b_re[j], s5_b_im[j],
                           s5_c_re[j], s5_c_im[j], s5_d[j], s5_w_glu[j], ln_g[i, 0], ln_b[i, 0], alpha)
            x2 = _moe_layer(xt, moe_w_router[j], moe_b_router[j], moe_w1[j], moe_w3[j], moe_w2[j],
                            ln_g[i, 1], ln_b[i, 1], alpha)
    return x2.reshape(bsz, seq, d)
```

```python
import functools

import jax
import jax.numpy as jnp
from jax import lax
from jax.experimental import pallas as pl
from jax.experimental.pallas import tpu as pltpu

F32 = jnp.float32
BF16 = jnp.bfloat16

CHUNK = 64
GDN_HEADS = 8
GDN_DK = 128
GDN_CONV = 4
S5_GROUP = 16
S5_TBLK = 16
COMBINE_FRAMES = 8
DISPATCH_BUFFERS = 3
LN_EPS = 1e-5
NORM_EPS = 1e-6

VMEM_LIMIT_BYTES = 56 * 1024 * 1024
CONV_CARRY_ROWS = 8
MXU_COLS = 256
PREP_CHUNKS = 4


def _params(*semantics):
    return pltpu.CompilerParams(dimension_semantics=semantics, vmem_limit_bytes=VMEM_LIMIT_BYTES)


def _const_spec(shape):
    nd = len(shape)
    return pl.BlockSpec(shape, lambda *_: (0,) * nd, pipeline_mode=pl.Buffered(1))


def _dot(a, b):
    return jnp.dot(a.astype(BF16), b.astype(BF16), preferred_element_type=F32)


def _dot_nt(a, b):
    return lax.dot_general(a.astype(BF16), b.astype(BF16), (((1,), (1,)), ((), ())),
                           preferred_element_type=F32)


def _dot_f32(a, b):
    return jnp.dot(a, b, preferred_element_type=F32, precision=lax.Precision.HIGHEST)


def _silu(x):
    return x * jax.nn.sigmoid(x)


def _gelu(x):
    return 0.5 * x * (1.0 + lax.erf(x * (2.0 ** -0.5)))


def _softplus(x):
    return jnp.maximum(x, 0.0) + jnp.log1p(jnp.exp(-jnp.abs(x)))


def _layer_norm(y, g, b):
    mu = jnp.mean(y, axis=-1, keepdims=True)
    yc = y - mu
    var = jnp.mean(yc * yc, axis=-1, keepdims=True)
    return yc * lax.rsqrt(var + LN_EPS) * g + b


def _gdn_in_kernel(x_ref, wqkv_ref, wz_ref, wb_ref, wa_ref, waT_ref, conv_ref, alog_ref, dtb_ref,
                   alogc_ref, dtbc_ref, tri_ref, triT_ref,
                   q_ref, k_ref, v_ref, z_ref, beta_ref, gc_ref, gcT_ref, carry_ref,
                   *, tiles_per_seq, tm, n_qk):
    i = pl.program_id(0)

    @pl.when(i % tiles_per_seq == 0)
    def _():
        carry_ref[...] = jnp.zeros(carry_ref.shape, F32)

    xb = x_ref[...].astype(BF16)
    z_ref[...] = jnp.dot(xb, wz_ref[...], preferred_element_type=F32).astype(z_ref.dtype)
    beta_ref[...] = jax.nn.sigmoid(jnp.dot(xb, wb_ref[...], preferred_element_type=F32))

    g_col = -jnp.exp(alog_ref[...]) * _softplus(jnp.dot(xb, wa_ref[...], preferred_element_type=F32) + dtb_ref[...])
    for c in range(tm // CHUNK):
        rows = slice(c * CHUNK, (c + 1) * CHUNK)
        gc_ref[rows, :] = _dot_f32(tri_ref[...], g_col[rows, :])
        a_row = lax.dot_general(waT_ref[...], xb[rows, :], (((1,), (1,)), ((), ())), preferred_element_type=F32)
        g_row = -jnp.exp(alogc_ref[...]) * _softplus(a_row + dtbc_ref[...])
        gcT_ref[c] = _dot_f32(g_row, triT_ref[...])

    for blk in range(carry_ref.shape[1] // MXU_COLS):
        cols = slice(blk * MXU_COLS, (blk + 1) * MXU_COLS)
        p = jnp.dot(xb, wqkv_ref[:, cols], preferred_element_type=F32)
        ext = jnp.concatenate([carry_ref[:, cols], p], axis=0)
        carry_ref[:, cols] = p[tm - CONV_CARRY_ROWS:tm, :]
        acc = conv_ref[GDN_CONV - 1:GDN_CONV, cols] * p
        for j in range(1, GDN_CONV):
            shifted = pltpu.roll(ext, j, axis=0)[CONV_CARRY_ROWS:, :]
            acc = acc + conv_ref[GDN_CONV - 1 - j:GDN_CONV - j, cols] * shifted
        y = _silu(acc)
        for hh in range(MXU_COLS // GDN_DK):
            head = blk * (MXU_COLS // GDN_DK) + hh
            yh = y[:, hh * GDN_DK:(hh + 1) * GDN_DK]
            if head < 2 * n_qk:
                yh = yh * lax.rsqrt(jnp.sum(yh * yh, axis=-1, keepdims=True) + NORM_EPS)
            if head < n_qk:
                q_ref[:, head * GDN_DK:(head + 1) * GDN_DK] = (yh * (GDN_DK ** -0.5)).astype(q_ref.dtype)
            elif head < 2 * n_qk:
                k_ref[:, (head - n_qk) * GDN_DK:(head - n_qk + 1) * GDN_DK] = yh.astype(k_ref.dtype)
            else:
                v_ref[:, (head - 2 * n_qk) * GDN_DK:(head - 2 * n_qk + 1) * GDN_DK] = yh.astype(v_ref.dtype)


def _gdn_in(x2, w_in, conv_w, a_log, dt_bias, seq, tm=256):
    t, d = x2.shape
    tm = min(tm, seq)
    nh = GDN_HEADS
    qk = nh * GDN_DK
    dv = (w_in.shape[1] - 2 * qk - 2 * nh) // 2
    wqkv = w_in[:, :2 * qk + dv].astype(BF16)
    wz = w_in[:, 2 * qk + dv:2 * qk + 2 * dv].astype(BF16)
    wb = w_in[:, 2 * qk + 2 * dv:2 * qk + 2 * dv + nh].astype(BF16)
    wa = w_in[:, 2 * qk + 2 * dv + nh:].astype(BF16)
    idx = jnp.arange(CHUNK)
    tri = (idx[:, None] >= idx[None, :]).astype(F32)
    n_chunks = t // CHUNK
    kern = functools.partial(_gdn_in_kernel, tiles_per_seq=seq // tm, tm=tm, n_qk=nh)
    row = lambda i: (i, 0)
    return pl.pallas_call(
        kern,
        grid=(t // tm,),
        in_specs=[
            pl.BlockSpec((tm, d), row),
            _const_spec(wqkv.shape), _const_spec(wz.shape), _const_spec(wb.shape), _const_spec(wa.shape),
            _const_spec((nh, d)), _const_spec(conv_w.shape),
            _const_spec((1, nh)), _const_spec((1, nh)), _const_spec((nh, 1)), _const_spec((nh, 1)),
            _const_spec((CHUNK, CHUNK)), _const_spec((CHUNK, CHUNK)),
        ],
        out_specs=[
            pl.BlockSpec((tm, qk), row), pl.BlockSpec((tm, qk), row), pl.BlockSpec((tm, dv), row),
            pl.BlockSpec((tm, dv), row), pl.BlockSpec((tm, nh), row), pl.BlockSpec((tm, nh), row),
            pl.BlockSpec((tm // CHUNK, nh, CHUNK), lambda i: (i, 0, 0)),
        ],
        out_shape=[
            jax.ShapeDtypeStruct((t, qk), BF16), jax.ShapeDtypeStruct((t, qk), BF16),
            jax.ShapeDtypeStruct((t, dv), BF16), jax.ShapeDtypeStruct((t, dv), BF16),
            jax.ShapeDtypeStruct((t, nh), F32), jax.ShapeDtypeStruct((t, nh), F32),
            jax.ShapeDtypeStruct((n_chunks, nh, CHUNK), F32),
        ],
        scratch_shapes=[pltpu.VMEM((CONV_CARRY_ROWS, 2 * qk + dv), F32)],
        compiler_params=_params("arbitrary"),
        name="gdn_in",
    )(x2, wqkv, wz, wb, wa, wa.T, conv_w.astype(F32),
      a_log.reshape(1, nh).astype(F32), dt_bias.reshape(1, nh).astype(F32),
      a_log.reshape(nh, 1).astype(F32), dt_bias.reshape(nh, 1).astype(F32), tri, tri.T)


def _gdn_core_kernel(q_ref, k_ref, v_ref, z_ref, beta_ref, gc_ref, gcT_ref, ng_ref, o_ref,
                     s_ref, u_ref, wq_ref, kd_ref, a_ref, *, n_sub):
    bsz, n_pair = s_ref.shape[0], s_ref.shape[1]
    pw = 2 * GDN_DK
    cc = CHUNK

    @pl.when(pl.program_id(0) == 0)
    def _():
        s_ref[...] = jnp.zeros(s_ref.shape, F32)

    r = lax.broadcasted_iota(jnp.int32, (cc, 2 * cc), 0)
    lane_m = lax.broadcasted_iota(jnp.int32, (cc, 2 * cc), 1)
    c = lane_m & (cc - 1)
    causal = r >= c
    strict = r > c
    eye = jnp.where(r == c, 1.0, 0.0)
    first_m = lane_m < cc
    first_w = lax.broadcasted_iota(jnp.int32, (cc, pw), 1) < GDN_DK
    sr = lax.broadcasted_iota(jnp.int32, (pw, pw), 0) < GDN_DK
    sc = lax.broadcasted_iota(jnp.int32, (pw, pw), 1) < GDN_DK
    same_head = sr == sc
    first_row = lax.broadcasted_iota(jnp.int32, (pw, 1), 0) < GDN_DK

    def block_diag(x, first):
        return jnp.concatenate([jnp.where(first, x, 0.0), jnp.where(first, 0.0, x)], axis=0)

    def pair_cols(col8, p, first):
        return jnp.where(first, col8[:, 2 * p:2 * p + 1], col8[:, 2 * p + 1:2 * p + 2])

    def prep(it, carry):
        chains = []
        for j in range(PREP_CHUNKS):
            idx = it * PREP_CHUNKS + j
            b, ci = idx // n_sub, idx % n_sub
            rows = pl.ds(pl.multiple_of(ci * cc, cc), cc)
            beta, gc, gct = beta_ref[b, rows, :], gc_ref[b, rows, :], gcT_ref[b, ci]
            for p in range(n_pair):
                chains.append((b, ci, rows, p, beta, gc, gct))
        ch = range(len(chains))
        lanes = [slice(p * pw, (p + 1) * pw) for (_, _, _, p, _, _, _) in chains]
        q = [q_ref[b, rows, lanes[i]].astype(F32) for i, (b, _, rows, *_) in enumerate(chains)]
        k = [k_ref[b, rows, lanes[i]].astype(F32) for i, (b, _, rows, *_) in enumerate(chains)]
        v = [v_ref[b, rows, lanes[i]].astype(F32) for i, (b, _, rows, *_) in enumerate(chains)]
        g_m = [pair_cols(gc, p, first_m) for (_, _, _, p, _, gc, _) in chains]
        g_w = [pair_cols(gc, p, first_w) for (_, _, _, p, _, gc, _) in chains]
        b_w = [pair_cols(beta, p, first_w) for (_, _, _, p, beta, _, _) in chains]
        decay = [jnp.exp(jnp.where(causal, g_m[i] - chains[i][6][chains[i][3]:chains[i][3] + 1, :], -jnp.inf)) for i in ch]
        eg = [jnp.exp(g_w[i]) for i in ch]
        kb = [k[i] * b_w[i] for i in ch]
        k_bd = [block_diag(k[i], first_w) for i in ch]
        m = [jnp.where(strict, _dot_nt(kb[i], k_bd[i]) * decay[i], 0.0) for i in ch]
        qk = [_dot_nt(q[i], k_bd[i]) for i in ch]
        t_inv = [eye - m[i] for i in ch]
        step = 1
        while 2 * step <= cc // 2:
            m = [_dot(m[i], block_diag(m[i], first_m)) for i in ch]
            t_inv = [t_inv[i] + _dot(t_inv[i], block_diag(m[i], first_m)) for i in ch]
            step *= 2
        rhs = [jnp.concatenate([block_diag(v[i] * b_w[i], first_w), block_diag(kb[i] * eg[i], first_w)], axis=1)
               for i in ch]
        uw = [_dot(t_inv[i], rhs[i]) for i in ch]
        for i, (b, ci, _, p, _, _, _) in enumerate(chains):
            u_ref[b, ci, p] = uw[i][:, :pw]
            wq_ref[b, ci, p, 0:cc, :] = uw[i][:, pw:].astype(BF16)
            wq_ref[b, ci, p, cc:2 * cc, :] = (q[i] * eg[i]).astype(BF16)
            kd_ref[b, ci, p] = (k[i] * jnp.exp(g_w[i][cc - 1:cc, :] - g_w[i])).T.astype(BF16)
            a_ref[b, ci, p] = (qk[i] * decay[i]).astype(BF16)
        return carry

    lax.fori_loop(0, bsz * n_sub // PREP_CHUNKS, prep, 0)

    chains = [(b, p) for b in range(bsz) for p in range(n_pair)]
    ch = range(len(chains))

    def scan(ci, carry):
        rows = pl.ds(pl.multiple_of(ci * cc, cc), cc)
        e_last = [jnp.exp(gc_ref[b, pl.ds(ci * cc + cc - 1, 1), :]) for b in range(bsz)]
        s = [s_ref[b, p] for (b, p) in chains]
        ws = [jnp.dot(wq_ref[b, ci, p], s[i].astype(BF16), preferred_element_type=F32) for i, (b, p) in enumerate(chains)]
        v_new = [u_ref[b, ci, p] - ws[i][:cc] for i, (b, p) in enumerate(chains)]
        o = [ws[i][cc:] + jnp.dot(a_ref[b, ci, p], block_diag(v_new[i], first_w).astype(BF16),
                                  preferred_element_type=F32) for i, (b, p) in enumerate(chains)]
        upd = [_dot(kd_ref[b, ci, p], v_new[i]) for i, (b, p) in enumerate(chains)]
        for i, (b, p) in enumerate(chains):
            e_col = jnp.where(first_row, e_last[b][:, 2 * p:2 * p + 1], e_last[b][:, 2 * p + 1:2 * p + 2])
            s_ref[b, p] = s[i] * e_col + jnp.where(same_head, upd[i], 0.0)
            for hh in range(2):
                cols = slice((2 * p + hh) * GDN_DK, (2 * p + hh + 1) * GDN_DK)
                oh = o[i][:, hh * GDN_DK:(hh + 1) * GDN_DK]
                on = oh * lax.rsqrt(jnp.mean(oh * oh, axis=-1, keepdims=True) + NORM_EPS) * ng_ref[...]
                o_ref[b, rows, cols] = (on * _silu(z_ref[b, rows, cols].astype(F32))).astype(o_ref.dtype)
        return carry

    lax.fori_loop(0, n_sub, scan, 0)


def _gdn_core(q, k, v, z, beta, gc, gct, norm_g, bsz, seq, tc=256):
    t, qk = q.shape
    tc = min(tc, seq)
    nh = GDN_HEADS
    dv = v.shape[1]
    n_sub = tc // CHUNK
    n_pair = nh // 2
    pw = 2 * GDN_DK
    blk3 = lambda w: pl.BlockSpec((bsz, tc, w), lambda i: (0, i, 0))
    seq3 = lambda a: a.reshape(bsz, seq, a.shape[-1])
    o = pl.pallas_call(
        functools.partial(_gdn_core_kernel, n_sub=n_sub),
        grid=(seq // tc,),
        in_specs=[blk3(qk), blk3(qk), blk3(dv), blk3(dv), blk3(nh), blk3(nh),
                  pl.BlockSpec((bsz, n_sub, n_pair, 2 * CHUNK), lambda i: (0, i, 0, 0)),
                  _const_spec((1, dv // nh))],
        out_specs=blk3(dv),
        out_shape=jax.ShapeDtypeStruct((bsz, seq, dv), BF16),
        scratch_shapes=[pltpu.VMEM((bsz, n_pair, pw, pw), F32),
                        pltpu.VMEM((bsz, n_sub, n_pair, CHUNK, pw), F32),
                        pltpu.VMEM((bsz, n_sub, n_pair, 2 * CHUNK, pw), BF16),
                        pltpu.VMEM((bsz, n_sub, n_pair, pw, CHUNK), BF16),
                        pltpu.VMEM((bsz, n_sub, n_pair, CHUNK, 2 * CHUNK), BF16)],
        compiler_params=_params("arbitrary"),
        name="gdn_core",
    )(seq3(q), seq3(k), seq3(v), seq3(z), seq3(beta), seq3(gc),
      gct.reshape(bsz, seq // CHUNK, n_pair, 2 * CHUNK), norm_g.reshape(1, -1).astype(F32))
    return o.reshape(t, dv)


def _proj_ln_kernel(a_ref, w_ref, x_ref, g_ref, b_ref, o_ref, *, alpha):
    h = jnp.dot(a_ref[...].astype(BF16), w_ref[...], preferred_element_type=F32)
    o_ref[...] = _layer_norm(alpha * x_ref[...] + h, g_ref[...], b_ref[...])


def _proj_ln(a, w, x2, ln_g, ln_b, alpha, tm=512):
    t, d = x2.shape
    tm = min(tm, t)
    row = lambda i: (i, 0)
    return pl.pallas_call(
        functools.partial(_proj_ln_kernel, alpha=alpha),
        grid=(t // tm,),
        in_specs=[pl.BlockSpec((tm, a.shape[1]), row), _const_spec(w.shape), pl.BlockSpec((tm, d), row),
                  _const_spec((1, d)), _const_spec((1, d))],
        out_specs=pl.BlockSpec((tm, d), row),
        out_shape=jax.ShapeDtypeStruct((t, d), F32),
        compiler_params=_params("arbitrary"),
        name="proj_ln",
    )(a, w.astype(BF16), x2, ln_g.reshape(1, d), ln_b.reshape(1, d))


def _ffn_ln_kernel(x_ref, w1_ref, w3_ref, w2_ref, g_ref, b_ref, o_ref, *, alpha, chunk_major):
    x = x_ref[...]
    xb = x.astype(BF16)
    hid = _silu(jnp.dot(xb, w1_ref[...], preferred_element_type=F32)) * jnp.dot(xb, w3_ref[...], preferred_element_type=F32)
    f = jnp.dot(hid.astype(BF16), w2_ref[...], preferred_element_type=F32)
    y = _layer_norm(alpha * x + f, g_ref[...], b_ref[...])
    if chunk_major:
        for c in range(o_ref.shape[1]):
            o_ref[:, c, :] = y[c * CHUNK:(c + 1) * CHUNK, :]
    else:
        o_ref[...] = y


def _ffn_ln(x2, w1, w3, w2, ln_g, ln_b, alpha, chunk_major=False, tm=512):
    t, d = x2.shape
    tm = min(tm, t)
    row = lambda i: (i, 0)
    if chunk_major:
        out_spec = pl.BlockSpec((CHUNK, tm // CHUNK, d), lambda i: (0, i, 0))
        out_shape = jax.ShapeDtypeStruct((CHUNK, t // CHUNK, d), F32)
    else:
        out_spec = pl.BlockSpec((tm, d), row)
        out_shape = jax.ShapeDtypeStruct((t, d), F32)
    return pl.pallas_call(
        functools.partial(_ffn_ln_kernel, alpha=alpha, chunk_major=chunk_major),
        grid=(t // tm,),
        in_specs=[pl.BlockSpec((tm, d), row), _const_spec(w1.shape), _const_spec(w3.shape), _const_spec(w2.shape),
                  _const_spec((1, d)), _const_spec((1, d))],
        out_specs=out_spec,
        out_shape=out_shape,
        compiler_params=_params("arbitrary"),
        name="ffn_ln",
    )(x2, w1.astype(BF16), w3.astype(BF16), w2.astype(BF16), ln_g.reshape(1, d), ln_b.reshape(1, d))


def _gdn_layer(x2, bsz, seq, w_in, conv_w, a_log, dt_bias, norm_g, w_out, ln_g, ln_b, alpha):
    q, k, v, z, beta, gc, gct = _gdn_in(x2, w_in, conv_w, a_log, dt_bias, seq)
    o = _gdn_core(q, k, v, z, beta, gc, gct, norm_g, bsz, seq)
    return _proj_ln(o, w_out, x2, ln_g, ln_b, alpha)


def _s5_taps_kernel(ce_ref, b_ref, o_ref):
    o_ref[0] = _dot_f32(ce_ref[0], b_ref[0])


def _s5_tables(lam_re, lam_im, log_dt, b_re, b_im, c_re, c_im, d_skip, n_per_seq):
    g, p = lam_re.shape
    gs = b_re.shape[-1]
    lr, li = lam_re.astype(F32), lam_im.astype(F32)
    dt = jnp.exp(log_dt.astype(F32))[:, None]
    zr, zi = lr * dt, li * dt

    def a_pow(steps):
        s = steps.astype(F32)[:, None, None]
        mag = jnp.exp(zr[None] * s)
        return mag * jnp.cos(zi[None] * s), mag * jnp.sin(zi[None] * s)

    ar, ai = a_pow(jnp.arange(CHUNK + 1))
    nr, ni = ar[1] - 1.0, ai[1]
    den = lr * lr + li * li
    qr, qi = (nr * lr + ni * li) / den, (ni * lr - nr * li) / den
    bbr = qr[..., None] * b_re - qi[..., None] * b_im
    bbi = qr[..., None] * b_im + qi[..., None] * b_re
    cr, ci = c_re.astype(F32), c_im.astype(F32)
    car = cr[:, None] * jnp.moveaxis(ar, 0, 1)[:, :, None, :] - ci[:, None] * jnp.moveaxis(ai, 0, 1)[:, :, None, :]
    cai = cr[:, None] * jnp.moveaxis(ai, 0, 1)[:, :, None, :] + ci[:, None] * jnp.moveaxis(ar, 0, 1)[:, :, None, :]
    ca = jnp.concatenate([car, -cai], axis=-1)
    ce = ca[:, :CHUNK].reshape(g, CHUNK * gs, 2 * p)
    bst = jnp.concatenate([bbr, bbi], axis=1)
    taps = pl.pallas_call(
        _s5_taps_kernel,
        grid=(g,),
        in_specs=[pl.BlockSpec((1, CHUNK * gs, 2 * p), lambda i: (i, 0, 0)),
                  pl.BlockSpec((1, 2 * p, gs), lambda i: (i, 0, 0))],
        out_specs=pl.BlockSpec((1, CHUNK * gs, gs), lambda i: (i, 0, 0)),
        out_shape=jax.ShapeDtypeStruct((g, CHUNK * gs, gs), F32),
        compiler_params=_params("arbitrary"),
        name="s5_taps",
    )(ce, bst).reshape(g, CHUNK, gs, gs)
    taps = taps.at[:, 0].add(d_skip.astype(F32).reshape(g, gs)[:, :, None] * jnp.eye(gs, dtype=F32))
    nb = CHUNK // S5_TBLK
    blk = jnp.arange(S5_TBLK)
    lag = S5_TBLK * jnp.arange(nb)[:, None, None] + blk[None, :, None] - blk[None, None, :]
    stk = jnp.where((lag >= 0)[None, :, :, :, None, None], taps[:, jnp.clip(lag, 0, CHUNK - 1)], 0.0)
    m_mat = jnp.transpose(stk, (0, 1, 2, 4, 3, 5)).reshape(g, CHUNK * gs, S5_TBLK * gs).astype(BF16)
    arr, aii = jnp.moveaxis(ar[:CHUNK][::-1], 0, 1), jnp.moveaxis(ai[:CHUNK][::-1], 0, 1)
    wr = arr[:, :, :, None] * bbr[:, None] - aii[:, :, :, None] * bbi[:, None]
    wi = arr[:, :, :, None] * bbi[:, None] + aii[:, :, :, None] * bbr[:, None]
    w_mat = jnp.concatenate([jnp.transpose(wr, (0, 2, 1, 3)), jnp.transpose(wi, (0, 2, 1, 3))], axis=1)
    w_mat = w_mat.reshape(g, 2 * p, CHUNK * gs).astype(BF16)
    v_mat = ca[:, 1:].reshape(g, CHUNK * gs, 2 * p).astype(BF16)
    n_steps = max(1, (n_per_seq - 1).bit_length())
    apr, api = a_pow(CHUNK * 2 ** jnp.arange(n_steps))
    apr = jnp.moveaxis(apr, 0, 1)[..., None]
    api = jnp.moveaxis(api, 0, 1)[..., None]
    return m_mat, w_mat, v_mat, apr, api


def _s5_in_kernel(x_ref, wT_ref, o_ref):
    o_ref[0] = _dot_nt(wT_ref[...], x_ref[0]).astype(o_ref.dtype)


def _s5_scan_kernel(u_ref, m_ref, w_ref, v_ref, apr_ref, api_ref, o_ref, *, n_per_seq):
    rows, gs, cols = u_ref.shape
    p = apr_ref.shape[2]
    u = u_ref[...].reshape(rows * gs, cols)
    hin = jnp.dot(w_ref[0], u, preferred_element_type=F32)
    yr, yi = hin[:p], hin[p:]
    pos = lax.broadcasted_iota(jnp.int32, (p, cols), 1) % n_per_seq

    def shifted(a, d):
        return jnp.where(pos >= d, pltpu.roll(a, d, axis=1), 0.0)

    d, kk = 1, 0
    while d < n_per_seq:
        sr, si = shifted(yr, d), shifted(yi, d)
        ar, ai = apr_ref[0, kk], api_ref[0, kk]
        yr, yi = yr + ar * sr - ai * si, yi + ar * si + ai * sr
        d, kk = 2 * d, kk + 1
    h_prev = jnp.concatenate([shifted(yr, 1), shifted(yi, 1)], axis=0).astype(BF16)
    nb = rows // S5_TBLK
    br = S5_TBLK * gs
    ys = [jnp.dot(v_ref[0, b * br:(b + 1) * br, :], h_prev, preferred_element_type=F32) for b in range(nb)]
    for sb in range(nb):
        part = jnp.dot(m_ref[0, 0:(nb - sb) * br, :], u[sb * br:(sb + 1) * br, :], preferred_element_type=F32)
        for j in range(nb - sb):
            ys[sb + j] = ys[sb + j] + part[j * br:(j + 1) * br, :]
    y = jnp.concatenate(ys, axis=0)
    o_ref[0] = _gelu(y).reshape(rows, gs, cols).astype(o_ref.dtype)


def _s5_out_kernel(h_ref, w_ref, x_ref, g_ref, b_ref, o_ref, *, alpha):
    d = x_ref.shape[-1]
    ht = h_ref[...].reshape(d, h_ref.shape[-1])
    vg = lax.dot_general(ht, w_ref[...], (((0,), (0,)), ((), ())), preferred_element_type=F32)
    o_ref[0] = _layer_norm(alpha * x_ref[0] + vg[:, :d] * jax.nn.sigmoid(vg[:, d:]), g_ref[...], b_ref[...])


def _s5_layer(xt, seq, w_in, lam_re, lam_im, log_dt, b_re, b_im, c_re, c_im, d_skip, w_glu, ln_g, ln_b, alpha):
    _, cols, d = xt.shape
    n_per_seq = seq // CHUNK
    g, p = lam_re.shape
    gs = d // g
    m_mat, w_mat, v_mat, apr, api = _s5_tables(lam_re, lam_im, log_dt, b_re, b_im, c_re, c_im, d_skip, n_per_seq)
    u_t = pl.pallas_call(
        _s5_in_kernel,
        grid=(CHUNK,),
        in_specs=[pl.BlockSpec((1, cols, d), lambda i: (i, 0, 0)), _const_spec((d, d))],
        out_specs=pl.BlockSpec((1, d, cols), lambda i: (i, 0, 0)),
        out_shape=jax.ShapeDtypeStruct((CHUNK, d, cols), BF16),
        compiler_params=_params("arbitrary"),
        name="s5_in",
    )(xt, w_in.T.astype(BF16))
    n_steps = apr.shape[1]
    hid = pl.pallas_call(
        functools.partial(_s5_scan_kernel, n_per_seq=n_per_seq),
        grid=(g,),
        in_specs=[pl.BlockSpec((CHUNK, gs, cols), lambda i: (0, i, 0)),
                  pl.BlockSpec((1, CHUNK * gs, S5_TBLK * gs), lambda i: (i, 0, 0)),
                  pl.BlockSpec((1, 2 * p, CHUNK * gs), lambda i: (i, 0, 0)),
                  pl.BlockSpec((1, CHUNK * gs, 2 * p), lambda i: (i, 0, 0)),
                  pl.BlockSpec((1, n_steps, p, 1), lambda i: (i, 0, 0, 0)),
                  pl.BlockSpec((1, n_steps, p, 1), lambda i: (i, 0, 0, 0))],
        out_specs=pl.BlockSpec((1, CHUNK, gs, cols), lambda i: (i, 0, 0, 0)),
        out_shape=jax.ShapeDtypeStruct((g, CHUNK, gs, cols), BF16),
        compiler_params=_params("arbitrary"),
        name="s5_scan",
    )(u_t, m_mat, w_mat, v_mat, apr, api)
    return pl.pallas_call(
        functools.partial(_s5_out_kernel, alpha=alpha),
        grid=(CHUNK,),
        in_specs=[pl.BlockSpec((g, 1, gs, cols), lambda i: (0, i, 0, 0)), _const_spec(w_glu.shape),
                  pl.BlockSpec((1, cols, d), lambda i: (i, 0, 0)), _const_spec((1, d)), _const_spec((1, d))],
        out_specs=pl.BlockSpec((1, cols, d), lambda i: (i, 0, 0)),
        out_shape=jax.ShapeDtypeStruct((CHUNK, cols, d), F32),
        compiler_params=_params("arbitrary"),
        name="s5_out",
    )(hid, w_glu.astype(BF16), xt, ln_g.reshape(1, d), ln_b.reshape(1, d))


def _top2_gates(logits):
    n_e = logits.shape[-1]
    lane = lax.broadcasted_iota(jnp.int32, logits.shape, 1)
    m1 = jnp.max(logits, axis=-1, keepdims=True)
    i1 = jnp.min(jnp.where(logits == m1, lane, n_e), axis=-1, keepdims=True)
    rest = jnp.where(lane == i1, -jnp.inf, logits)
    m2 = jnp.max(rest, axis=-1, keepdims=True)
    i2 = jnp.min(jnp.where(rest == m2, lane, n_e), axis=-1, keepdims=True)
    e2 = jnp.exp(m2 - m1)
    den = 1.0 + e2
    return i1, i2, 1.0 / den, e2 / den


def _moe_route_kernel(x_ref, wr_ref, br_ref, low_ref, idx_ref, gate_ref, cnt_ref, base_ref):
    @pl.when(pl.program_id(0) == 0)
    def _():
        base_ref[...] = jnp.zeros(base_ref.shape, F32)

    i1, i2, g1, g2 = _top2_gates(_dot_f32(x_ref[...], wr_ref[...]) + br_ref[...])
    lane = lax.broadcasted_iota(jnp.int32, (x_ref.shape[0], wr_ref.shape[1]), 1)
    sel = jnp.where((lane == i1) | (lane == i2), 1.0, 0.0)
    rank = base_ref[...] + jnp.dot(low_ref[...], sel.astype(BF16), preferred_element_type=F32)
    p1 = jnp.sum(jnp.where(lane == i1, rank, 0.0), axis=-1, keepdims=True).astype(jnp.int32)
    p2 = jnp.sum(jnp.where(lane == i2, rank, 0.0), axis=-1, keepdims=True).astype(jnp.int32)
    idx_ref[...] = jnp.where(lane == 0, i1, jnp.where(lane == 1, i2, jnp.where(lane == 2, p1, jnp.where(lane == 3, p2, 0))))
    gate_ref[...] = jnp.where(lane == 0, g1, jnp.where(lane == 1, g2, 0.0))
    base_ref[...] += jnp.sum(sel, axis=0, keepdims=True)
    cnt_ref[...] = base_ref[...].astype(jnp.int32)


def _row_gather_start(src_hbm, idx_ref, dst_ref, sem, n_rows):
    for r in range(n_rows):
        pltpu.make_async_copy(src_hbm.at[pl.ds(idx_ref[0, 0, r], 1)], dst_ref.at[pl.ds(r, 1)], sem).start()


def _row_gather_wait(src_hbm, dst_ref, sem, n_rows):
    pltpu.make_async_copy(src_hbm.at[pl.ds(0, n_rows)], dst_ref, sem).wait()


def _moe_dispatch_kernel(pad_start_ref, pad_len_ref, tail_ref, d1_ref, d2_ref, x_hbm, xs_hbm, xbuf, zbuf,
                         in_sem, sem, pad_sem):
    i = pl.program_id(0)
    n = pl.num_programs(0)
    n_buf, tb, _ = xbuf.shape

    def fetch(j, b):
        return pltpu.make_async_copy(x_hbm.at[pl.ds(pl.multiple_of(j * tb, tb), tb)], xbuf.at[b], in_sem.at[b])

    @pl.when(i == 0)
    def _():
        fetch(0, 0).start()
        zbuf[...] = jnp.zeros(zbuf.shape, F32)
        for e in range(pad_start_ref.shape[0]):
            def zero_row(k, carry, e=e):
                pltpu.make_async_copy(zbuf.at[pl.ds(0, 1)], xs_hbm.at[pl.ds(pad_start_ref[e] + k, 1)], pad_sem).start()
                return carry
            lax.fori_loop(0, pad_len_ref[e], zero_row, 0)

        def zero_piece(k, carry):
            row0 = pl.multiple_of(tail_ref[0] + k * tb, tb)
            pltpu.make_async_copy(zbuf, xs_hbm.at[pl.ds(row0, tb)], pad_sem).start()
            return carry
        lax.fori_loop(0, tail_ref[1], zero_piece, 0)
        for e in range(pad_start_ref.shape[0]):
            def wait_row(k, carry):
                pltpu.make_async_copy(zbuf.at[pl.ds(0, 1)], xs_hbm.at[pl.ds(0, 1)], pad_sem).wait()
                return carry
            lax.fori_loop(0, pad_len_ref[e], wait_row, 0)

        def wait_piece(k, carry):
            pltpu.make_async_copy(zbuf, xs_hbm.at[pl.ds(0, tb)], pad_sem).wait()
            return carry
        lax.fori_loop(0, tail_ref[1], wait_piece, 0)

    @pl.when(i + 1 < n)
    def _():
        fetch(i + 1, (i + 1) % n_buf).start()

    buf = i % n_buf
    fetch(i, buf).wait()
    slot = i % 2
    for r in range(tb):
        row = xbuf.at[buf, pl.ds(r, 1)]
        pltpu.make_async_copy(row, xs_hbm.at[pl.ds(d1_ref[0, 0, r], 1)], sem.at[slot]).start()
        pltpu.make_async_copy(row, xs_hbm.at[pl.ds(d2_ref[0, 0, r], 1)], sem.at[slot]).start()

    def wait_tile(s):
        for _ in range(2):
            pltpu.make_async_copy(xbuf.at[0], xs_hbm.at[pl.ds(0, tb)], sem.at[s]).wait()

    @pl.when(i > 0)
    def _():
        wait_tile(1 - slot)

    @pl.when(i == n - 1)
    def _():
        wait_tile(slot)


def _moe_expert_kernel(tile_e_ref, n_used_ref, xs_ref, w1_ref, w3_ref, w2_ref, y_ref):
    del tile_e_ref
    used = pl.program_id(0) < n_used_ref[0]

    @pl.when(used)
    def _():
        xb = xs_ref[...].astype(BF16)
        hid = _silu(jnp.dot(xb, w1_ref[0], preferred_element_type=F32)) * jnp.dot(xb, w3_ref[0], preferred_element_type=F32)
        y_ref[...] = jnp.dot(hid.astype(BF16), w2_ref[0], preferred_element_type=F32)

    @pl.when(jnp.logical_not(used))
    def _():
        y_ref[...] = jnp.zeros(y_ref.shape, F32)


def _moe_combine_kernel(d1_ref, d2_ref, d1n_ref, d2n_ref, x_ref, gate_ref, y_hbm, g_ref, b_ref, o_ref, ybuf, sem,
                        *, alpha):
    i = pl.program_id(0)
    n = pl.num_programs(0)
    nf, cb, d = x_ref.shape
    tb = nf * cb
    slot = i % 2

    def start(a_ref, b_ref, s):
        _row_gather_start(y_hbm, a_ref, ybuf.at[s, 0], sem.at[s], tb)
        _row_gather_start(y_hbm, b_ref, ybuf.at[s, 1], sem.at[s], tb)

    @pl.when(i == 0)
    def _():
        start(d1_ref, d2_ref, 0)

    @pl.when(i + 1 < n)
    def _():
        start(d1n_ref, d2n_ref, 1 - slot)

    _row_gather_wait(y_hbm, ybuf.at[slot, 0], sem.at[slot], tb)
    _row_gather_wait(y_hbm, ybuf.at[slot, 1], sem.at[slot], tb)
    gate = gate_ref[...].reshape(tb, gate_ref.shape[-1])
    f = gate[:, 0:1] * ybuf[slot, 0] + gate[:, 1:2] * ybuf[slot, 1]
    y = _layer_norm(alpha * x_ref[...].reshape(tb, d) + f, g_ref[...], b_ref[...])
    for k in range(nf):
        o_ref[:, k, :] = y[k * cb:(k + 1) * cb, :]


def _moe_layer(xt, w_router, b_router, w1, w3, w2, ln_g, ln_b, alpha, tm=1024, tr=256, tb=256):
    n_f, n_c, d = xt.shape
    t = n_f * n_c
    x2 = xt.reshape(t, d)
    n_e, _, dff = w1.shape
    tm = min(tm, t)
    tb = min(tb, COMBINE_FRAMES * n_c)
    assert tr % tb == 0, "the grouped buffer's tail is zeroed in tb-row pieces"
    idx = jnp.arange(tm)
    low = (idx[:, None] > idx[None, :]).astype(BF16)
    route_i, route_g, counts = pl.pallas_call(
        _moe_route_kernel,
        grid=(t // tm,),
        in_specs=[pl.BlockSpec((tm, d), lambda i: (i, 0)), _const_spec((d, n_e)), _const_spec((1, n_e)),
                  _const_spec((tm, tm))],
        out_specs=[pl.BlockSpec((tm, n_e), lambda i: (i, 0)), pl.BlockSpec((tm, n_e), lambda i: (i, 0)),
                   pl.BlockSpec((1, n_e), lambda i: (0, 0))],
        out_shape=[jax.ShapeDtypeStruct((t, n_e), jnp.int32), jax.ShapeDtypeStruct((t, n_e), F32),
                   jax.ShapeDtypeStruct((1, n_e), jnp.int32)],
        scratch_shapes=[pltpu.VMEM((1, n_e), F32)],
        compiler_params=_params("arbitrary"),
        name="moe_route",
    )(x2, w_router.astype(F32), b_router.reshape(1, n_e).astype(F32), low)

    n_tiles = (2 * t) // tr + n_e
    cnt = counts[0]
    size = ((cnt + tr - 1) // tr) * tr
    ends = jnp.cumsum(size)
    offs = ends - size
    e1, e2, p1, p2 = route_i[:, 0], route_i[:, 1], route_i[:, 2], route_i[:, 3]
    dest1 = offs[e1] + p1
    dest2 = offs[e2] + p2
    tile_e = jnp.minimum(jnp.sum(jnp.arange(n_tiles, dtype=jnp.int32)[:, None] * tr >= ends[None, :], axis=1),
                         n_e - 1).astype(jnp.int32)
    n_used = (ends[n_e - 1:] // tr).astype(jnp.int32)

    smem_blk = lambda n: pl.BlockSpec((1, 1, n), lambda i, *_: (i, 0, 0), memory_space=pltpu.SMEM)
    smem_next = lambda n, last: pl.BlockSpec((1, 1, n), lambda i, *_: (jnp.minimum(i + 1, last), 0, 0),
                                             memory_space=pltpu.SMEM)
    any_spec = pl.BlockSpec(memory_space=pl.ANY)
    xs = pl.pallas_call(
        _moe_dispatch_kernel,
        grid_spec=pltpu.PrefetchScalarGridSpec(
            num_scalar_prefetch=3,
            grid=(t // tb,),
            in_specs=[smem_blk(tb), smem_blk(tb), any_spec],
            out_specs=any_spec,
            scratch_shapes=[pltpu.VMEM((DISPATCH_BUFFERS, tb, d), F32), pltpu.VMEM((tb, d), F32),
                            pltpu.SemaphoreType.DMA((DISPATCH_BUFFERS,)), pltpu.SemaphoreType.DMA((2,)),
                            pltpu.SemaphoreType.DMA(())],
        ),
        out_shape=jax.ShapeDtypeStruct((n_tiles * tr, d), F32),
        compiler_params=_params("arbitrary"),
        name="moe_dispatch",
    )((offs + cnt).astype(jnp.int32), (size - cnt).astype(jnp.int32),
      jnp.concatenate([ends[n_e - 1:], (n_tiles * tr - ends[n_e - 1:]) // tb]).astype(jnp.int32),
      dest1.reshape(t // tb, 1, tb), dest2.reshape(t // tb, 1, tb), x2)

    tile = lambda i, te, nu: (i, 0)
    wspec = lambda shape: pl.BlockSpec(shape, lambda i, te, nu: (te[i], 0, 0))
    ys = pl.pallas_call(
        _moe_expert_kernel,
        grid_spec=pltpu.PrefetchScalarGridSpec(
            num_scalar_prefetch=2,
            grid=(n_tiles,),
            in_specs=[pl.BlockSpec((tr, d), tile), wspec((1, d, dff)), wspec((1, d, dff)), wspec((1, dff, d))],
            out_specs=pl.BlockSpec((tr, d), tile),
        ),
        out_shape=jax.ShapeDtypeStruct((n_tiles * tr, d), F32),
        compiler_params=_params("arbitrary"),
        name="moe_experts",
    )(tile_e, n_used, xs, w1.astype(BF16), w3.astype(BF16), w2.astype(BF16))

    nf = COMBINE_FRAMES
    cb = tb // nf
    n_fb, n_cb = n_f // nf, n_c // cb
    n_tb = n_fb * n_cb
    tile_order = lambda a: a.reshape(n_fb, nf, n_cb, cb).transpose(0, 2, 1, 3).reshape(n_tb, 1, tb)
    d1, d2 = tile_order(dest1), tile_order(dest2)
    in_blk = lambda i: (i // n_cb, i % n_cb, 0)
    out = pl.pallas_call(
        functools.partial(_moe_combine_kernel, alpha=alpha),
        grid=(n_tb,),
        in_specs=[smem_blk(tb), smem_blk(tb), smem_next(tb, n_tb - 1), smem_next(tb, n_tb - 1),
                  pl.BlockSpec((nf, cb, d), in_blk), pl.BlockSpec((nf, cb, n_e), in_blk),
                  pl.BlockSpec(memory_space=pl.ANY), _const_spec((1, d)), _const_spec((1, d))],
        out_specs=pl.BlockSpec((cb, nf, d), lambda i: (i % n_cb, i // n_cb, 0)),
        out_shape=jax.ShapeDtypeStruct((n_c, n_f, d), F32),
        scratch_shapes=[pltpu.VMEM((2, 2, tb, d), F32), pltpu.SemaphoreType.DMA((2,))],
        compiler_params=_params("arbitrary"),
        name="moe_combine",
    )(d1, d2, d1, d2, xt, route_g.reshape(n_f, n_c, n_e), ys, ln_g.reshape(1, d), ln_b.reshape(1, d))
    return out.reshape(t, d)


def kernel(x, gdn_w_in, gdn_conv_w, gdn_a_log, gdn_dt_bias, gdn_norm_g, gdn_w_out, ffn_w1, ffn_w3, ffn_w2,
           s5_w_in, s5_lam_re, s5_lam_im, s5_log_dt, s5_b_re, s5_b_im, s5_c_re, s5_c_im, s5_d, s5_w_glu,
           moe_w_router, moe_b_router, moe_w1, moe_w3, moe_w2, ln_g, ln_b):
    bsz, seq, d = x.shape
    depth = ln_g.shape[0]
    alpha = (2 * depth) ** 0.25
    x2 = x.reshape(bsz * seq, d)
    for i in range(depth):
        j = i // 2
        if i % 2 == 0:
            x2 = _gdn_layer(x2, bsz, seq, gdn_w_in[j], gdn_conv_w[j], gdn_a_log[j], gdn_dt_bias[j], gdn_norm_g[j],
                            gdn_w_out[j], ln_g[i, 0], ln_b[i, 0], alpha)
            x2 = _ffn_ln(x2, ffn_w1[j], ffn_w3[j], ffn_w2[j], ln_g[i, 1], ln_b[i, 1], alpha,
                         chunk_major=i + 1 < depth)
        else:
            xt = _s5_layer(x2, seq, s5_w_in[j], s5_lam_re[j], s5_lam_im[j], s5_log_dt[j], s5_b_re[j], s5_b_im[j],
                           s5_c_re[j], s5_c_im[j], s5_d[j], s5_w_glu[j], ln_g[i, 0], ln_b[i, 0], alpha)
            x2 = _moe_layer(xt, moe_w_router[j], moe_b_router[j], moe_w1[j], moe_w3[j], moe_w2[j],
                            ln_g[i, 1], ln_b[i, 1], alpha)
    return x2.reshape(bsz, seq, d)
```

```python
import functools

import jax
import jax.numpy as jnp
from jax import lax
from jax.experimental import pallas as pl
from jax.experimental.pallas import tpu as pltpu

F32 = jnp.float32
BF16 = jnp.bfloat16

CHUNK = 64
GDN_HEADS = 8
GDN_DK = 128
GDN_CONV = 4
S5_GROUP = 16
S5_TBLK = 16
COMBINE_FRAMES = 8
DISPATCH_BUFFERS = 3
LN_EPS = 1e-5
NORM_EPS = 1e-6

VMEM_LIMIT_BYTES = 56 * 1024 * 1024
CONV_CARRY_ROWS = 8
MXU_COLS = 256
PREP_CHUNKS = 4


def _params(*semantics):
    return pltpu.CompilerParams(dimension_semantics=semantics, vmem_limit_bytes=VMEM_LIMIT_BYTES)


def _const_spec(shape):
    nd = len(shape)
    return pl.BlockSpec(shape, lambda *_: (0,) * nd, pipeline_mode=pl.Buffered(1))


def _dot(a, b):
    return jnp.dot(a.astype(BF16), b.astype(BF16), preferred_element_type=F32)


def _dot_nt(a, b):
    return lax.dot_general(a.astype(BF16), b.astype(BF16), (((1,), (1,)), ((), ())),
                           preferred_element_type=F32)


def _dot_f32(a, b):
    return jnp.dot(a, b, preferred_element_type=F32, precision=lax.Precision.HIGHEST)


def _silu(x):
    return x * jax.nn.sigmoid(x)


def _gelu(x):
    return 0.5 * x * (1.0 + lax.erf(x * (2.0 ** -0.5)))


def _softplus(x):
    return jnp.maximum(x, 0.0) + jnp.log1p(jnp.exp(-jnp.abs(x)))


def _layer_norm(y, g, b):
    mu = jnp.mean(y, axis=-1, keepdims=True)
    yc = y - mu
    var = jnp.mean(yc * yc, axis=-1, keepdims=True)
    return yc * lax.rsqrt(var + LN_EPS) * g + b


def _gdn_in_kernel(x_ref, wqkv_ref, wz_ref, wb_ref, wa_ref, waT_ref, conv_ref, alog_ref, dtb_ref,
                   alogc_ref, dtbc_ref, tri_ref, triT_ref,
                   q_ref, k_ref, v_ref, z_ref, beta_ref, gc_ref, gcT_ref, carry_ref,
                   *, tiles_per_seq, tm, n_qk):
    i = pl.program_id(0)

    @pl.when(i % tiles_per_seq == 0)
    def _():
        carry_ref[...] = jnp.zeros(carry_ref.shape, F32)

    xb = x_ref[...].astype(BF16)
    z_ref[...] = jnp.dot(xb, wz_ref[...], preferred_element_type=F32).astype(z_ref.dtype)
    beta_ref[...] = jax.nn.sigmoid(jnp.dot(xb, wb_ref[...], preferred_element_type=F32))

    g_col = -jnp.exp(alog_ref[...]) * _softplus(jnp.dot(xb, wa_ref[...], preferred_element_type=F32) + dtb_ref[...])
    for c in range(tm // CHUNK):
        rows = slice(c * CHUNK, (c + 1) * CHUNK)
        gc_ref[rows, :] = _dot_f32(tri_ref[...], g_col[rows, :])
        a_row = lax.dot_general(waT_ref[...], xb[rows, :], (((1,), (1,)), ((), ())), preferred_element_type=F32)
        g_row = -jnp.exp(alogc_ref[...]) * _softplus(a_row + dtbc_ref[...])
        gcT_ref[c] = _dot_f32(g_row, triT_ref[...])

    for blk in range(carry_ref.shape[1] // MXU_COLS):
        cols = slice(blk * MXU_COLS, (blk + 1) * MXU_COLS)
        p = jnp.dot(xb, wqkv_ref[:, cols], preferred_element_type=F32)
        ext = jnp.concatenate([carry_ref[:, cols], p], axis=0)
        carry_ref[:, cols] = p[tm - CONV_CARRY_ROWS:tm, :]
        acc = conv_ref[GDN_CONV - 1:GDN_CONV, cols] * p
        for j in range(1, GDN_CONV):
            shifted = pltpu.roll(ext, j, axis=0)[CONV_CARRY_ROWS:, :]
            acc = acc + conv_ref[GDN_CONV - 1 - j:GDN_CONV - j, cols] * shifted
        y = _silu(acc)
        for hh in range(MXU_COLS // GDN_DK):
            head = blk * (MXU_COLS // GDN_DK) + hh
            yh = y[:, hh * GDN_DK:(hh + 1) * GDN_DK]
            if head < 2 * n_qk:
                yh = yh * lax.rsqrt(jnp.sum(yh * yh, axis=-1, keepdims=True) + NORM_EPS)
            if head < n_qk:
                q_ref[:, head * GDN_DK:(head + 1) * GDN_DK] = (yh * (GDN_DK ** -0.5)).astype(q_ref.dtype)
            elif head < 2 * n_qk:
                k_ref[:, (head - n_qk) * GDN_DK:(head - n_qk + 1) * GDN_DK] = yh.astype(k_ref.dtype)
            else:
                v_ref[:, (head - 2 * n_qk) * GDN_DK:(head - 2 * n_qk + 1) * GDN_DK] = yh.astype(v_ref.dtype)


def _gdn_in(x2, w_in, conv_w, a_log, dt_bias, seq, tm=256):
    t, d = x2.shape
    tm = min(tm, seq)
    nh = GDN_HEADS
    qk = nh * GDN_DK
    dv = (w_in.shape[1] - 2 * qk - 2 * nh) // 2
    wqkv = w_in[:, :2 * qk + dv].astype(BF16)
    wz = w_in[:, 2 * qk + dv:2 * qk + 2 * dv].astype(BF16)
    wb = w_in[:, 2 * qk + 2 * dv:2 * qk + 2 * dv + nh].astype(BF16)
    wa = w_in[:, 2 * qk + 2 * dv + nh:].astype(BF16)
    idx = jnp.arange(CHUNK)
    tri = (idx[:, None] >= idx[None, :]).astype(F32)
    n_chunks = t // CHUNK
    kern = functools.partial(_gdn_in_kernel, tiles_per_seq=seq // tm, tm=tm, n_qk=nh)
    row = lambda i: (i, 0)
    return pl.pallas_call(
        kern,
        grid=(t // tm,),
        in_specs=[
            pl.BlockSpec((tm, d), row),
            _const_spec(wqkv.shape), _const_spec(wz.shape), _const_spec(wb.shape), _const_spec(wa.shape),
            _const_spec((nh, d)), _const_spec(conv_w.shape),
            _const_spec((1, nh)), _const_spec((1, nh)), _const_spec((nh, 1)), _const_spec((nh, 1)),
            _const_spec((CHUNK, CHUNK)), _const_spec((CHUNK, CHUNK)),
        ],
        out_specs=[
            pl.BlockSpec((tm, qk), row), pl.BlockSpec((tm, qk), row), pl.BlockSpec((tm, dv), row),
            pl.BlockSpec((tm, dv), row), pl.BlockSpec((tm, nh), row), pl.BlockSpec((tm, nh), row),
            pl.BlockSpec((tm // CHUNK, nh, CHUNK), lambda i: (i, 0, 0)),
        ],
        out_shape=[
            jax.ShapeDtypeStruct((t, qk), BF16), jax.ShapeDtypeStruct((t, qk), BF16),
            jax.ShapeDtypeStruct((t, dv), BF16), jax.ShapeDtypeStruct((t, dv), BF16),
            jax.ShapeDtypeStruct((t, nh), F32), jax.ShapeDtypeStruct((t, nh), F32),
            jax.ShapeDtypeStruct((n_chunks, nh, CHUNK), F32),
        ],
        scratch_shapes=[pltpu.VMEM((CONV_CARRY_ROWS, 2 * qk + dv), F32)],
        compiler_params=_params("arbitrary"),
        name="gdn_in",
    )(x2, wqkv, wz, wb, wa, wa.T, conv_w.astype(F32),
      a_log.reshape(1, nh).astype(F32), dt_bias.reshape(1, nh).astype(F32),
      a_log.reshape(nh, 1).astype(F32), dt_bias.reshape(nh, 1).astype(F32), tri, tri.T)


def _gdn_core_kernel(q_ref, k_ref, v_ref, z_ref, beta_ref, gc_ref, gcT_ref, ng_ref, o_ref,
                     s_ref, u_ref, wq_ref, kd_ref, a_ref, *, n_sub):
    bsz, n_pair = s_ref.shape[0], s_ref.shape[1]
    pw = 2 * GDN_DK
    cc = CHUNK

    @pl.when(pl.program_id(0) == 0)
    def _():
        s_ref[...] = jnp.zeros(s_ref.shape, F32)

    r = lax.broadcasted_iota(jnp.int32, (cc, 2 * cc), 0)
    lane_m = lax.broadcasted_iota(jnp.int32, (cc, 2 * cc), 1)
    c = lane_m & (cc - 1)
    causal = r >= c
    strict = r > c
    eye = jnp.where(r == c, 1.0, 0.0)
    first_m = lane_m < cc
    first_w = lax.broadcasted_iota(jnp.int32, (cc, pw), 1) < GDN_DK
    sr = lax.broadcasted_iota(jnp.int32, (pw, pw), 0) < GDN_DK
    sc = lax.broadcasted_iota(jnp.int32, (pw, pw), 1) < GDN_DK
    same_head = sr == sc
    first_row = lax.broadcasted_iota(jnp.int32, (pw, 1), 0) < GDN_DK

    def block_diag(x, first):
        return jnp.concatenate([jnp.where(first, x, 0.0), jnp.where(first, 0.0, x)], axis=0)

    def pair_cols(col8, p, first):
        return jnp.where(first, col8[:, 2 * p:2 * p + 1], col8[:, 2 * p + 1:2 * p + 2])

    def prep(it, carry):
        chains = []
        for j in range(PREP_CHUNKS):
            idx = it * PREP_CHUNKS + j
            b, ci = idx // n_sub, idx % n_sub
            rows = pl.ds(pl.multiple_of(ci * cc, cc), cc)
            beta, gc, gct = beta_ref[b, rows, :], gc_ref[b, rows, :], gcT_ref[b, ci]
            for p in range(n_pair):
                chains.append((b, ci, rows, p, beta, gc, gct))
        ch = range(len(chains))
        lanes = [slice(p * pw, (p + 1) * pw) for (_, _, _, p, _, _, _) in chains]
        q = [q_ref[b, rows, lanes[i]].astype(F32) for i, (b, _, rows, *_) in enumerate(chains)]
        k = [k_ref[b, rows, lanes[i]].astype(F32) for i, (b, _, rows, *_) in enumerate(chains)]
        v = [v_ref[b, rows, lanes[i]].astype(F32) for i, (b, _, rows, *_) in enumerate(chains)]
        g_m = [pair_cols(gc, p, first_m) for (_, _, _, p, _, gc, _) in chains]
        g_w = [pair_cols(gc, p, first_w) for (_, _, _, p, _, gc, _) in chains]
        b_w = [pair_cols(beta, p, first_w) for (_, _, _, p, beta, _, _) in chains]
        decay = [jnp.exp(jnp.where(causal, g_m[i] - chains[i][6][chains[i][3]:chains[i][3] + 1, :], -jnp.inf)) for i in ch]
        eg = [jnp.exp(g_w[i]) for i in ch]
        kb = [k[i] * b_w[i] for i in ch]
        k_bd = [block_diag(k[i], first_w) for i in ch]
        m = [jnp.where(strict, _dot_nt(kb[i], k_bd[i]) * decay[i], 0.0) for i in ch]
        qk = [_dot_nt(q[i], k_bd[i]) for i in ch]
        t_inv = [eye - m[i] for i in ch]
        step = 1
        while 2 * step <= cc // 2:
            m = [_dot(m[i], block_diag(m[i], first_m)) for i in ch]
            t_inv = [t_inv[i] + _dot(t_inv[i], block_diag(m[i], first_m)) for i in ch]
            step *= 2
        rhs = [jnp.concatenate([block_diag(v[i] * b_w[i], first_w), block_diag(kb[i] * eg[i], first_w)], axis=1)
               for i in ch]
        uw = [_dot(t_inv[i], rhs[i]) for i in ch]
        for i, (b, ci, _, p, _, _, _) in enumerate(chains):
            u_ref[b, ci, p] = uw[i][:, :pw]
            wq_ref[b, ci, p, 0:cc, :] = uw[i][:, pw:].astype(BF16)
            wq_ref[b, ci, p, cc:2 * cc, :] = (q[i] * eg[i]).astype(BF16)
            kd_ref[b, ci, p] = (k[i] * jnp.exp(g_w[i][cc - 1:cc, :] - g_w[i])).T.astype(BF16)
            a_ref[b, ci, p] = (qk[i] * decay[i]).astype(BF16)
        return carry

    lax.fori_loop(0, bsz * n_sub // PREP_CHUNKS, prep, 0)

    chains = [(b, p) for b in range(bsz) for p in range(n_pair)]
    ch = range(len(chains))

    def scan(ci, carry):
        rows = pl.ds(pl.multiple_of(ci * cc, cc), cc)
        e_last = [jnp.exp(gc_ref[b, pl.ds(ci * cc + cc - 1, 1), :]) for b in range(bsz)]
        s = [s_ref[b, p] for (b, p) in chains]
        ws = [jnp.dot(wq_ref[b, ci, p], s[i].astype(BF16), preferred_element_type=F32) for i, (b, p) in enumerate(chains)]
        v_new = [u_ref[b, ci, p] - ws[i][:cc] for i, (b, p) in enumerate(chains)]
        o = [ws[i][cc:] + jnp.dot(a_ref[b, ci, p], block_diag(v_new[i], first_w).astype(BF16),
                                  preferred_element_type=F32) for i, (b, p) in enumerate(chains)]
        upd = [_dot(kd_ref[b, ci, p], v_new[i]) for i, (b, p) in enumerate(chains)]
        for i, (b, p) in enumerate(chains):
            e_col = jnp.where(first_row, e_last[b][:, 2 * p:2 * p + 1], e_last[b][:, 2 * p + 1:2 * p + 2])
            s_ref[b, p] = s[i] * e_col + jnp.where(same_head, upd[i], 0.0)
            for hh in range(2):
                cols = slice((2 * p + hh) * GDN_DK, (2 * p + hh + 1) * GDN_DK)
                oh = o[i][:, hh * GDN_DK:(hh + 1) * GDN_DK]
                on = oh * lax.rsqrt(jnp.mean(oh * oh, axis=-1, keepdims=True) + NORM_EPS) * ng_ref[...]
                o_ref[b, rows, cols] = (on * _silu(z_ref[b, rows, cols].astype(F32))).astype(o_ref.dtype)
        return carry

    lax.fori_loop(0, n_sub, scan, 0)


def _gdn_core(q, k, v, z, beta, gc, gct, norm_g, bsz, seq, tc=256):
    t, qk = q.shape
    tc = min(tc, seq)
    nh = GDN_HEADS
    dv = v.shape[1]
    n_sub = tc // CHUNK
    n_pair = nh // 2
    pw = 2 * GDN_DK
    blk3 = lambda w: pl.BlockSpec((bsz, tc, w), lambda i: (0, i, 0))
    seq3 = lambda a: a.reshape(bsz, seq, a.shape[-1])
    o = pl.pallas_call(
        functools.partial(_gdn_core_kernel, n_sub=n_sub),
        grid=(seq // tc,),
        in_specs=[blk3(qk), blk3(qk), blk3(dv), blk3(dv), blk3(nh), blk3(nh),
                  pl.BlockSpec((bsz, n_sub, n_pair, 2 * CHUNK), lambda i: (0, i, 0, 0)),
                  _const_spec((1, dv // nh))],
        out_specs=blk3(dv),
        out_shape=jax.ShapeDtypeStruct((bsz, seq, dv), BF16),
        scratch_shapes=[pltpu.VMEM((bsz, n_pair, pw, pw), F32),
                        pltpu.VMEM((bsz, n_sub, n_pair, CHUNK, pw), F32),
                        pltpu.VMEM((bsz, n_sub, n_pair, 2 * CHUNK, pw), BF16),
                        pltpu.VMEM((bsz, n_sub, n_pair, pw, CHUNK), BF16),
                        pltpu.VMEM((bsz, n_sub, n_pair, CHUNK, 2 * CHUNK), BF16)],
        compiler_params=_params("arbitrary"),
        name="gdn_core",
    )(seq3(q), seq3(k), seq3(v), seq3(z), seq3(beta), seq3(gc),
      gct.reshape(bsz, seq // CHUNK, n_pair, 2 * CHUNK), norm_g.reshape(1, -1).astype(F32))
    return o.reshape(t, dv)


def _proj_ln_kernel(a_ref, w_ref, x_ref, g_ref, b_ref, o_ref, *, alpha):
    h = jnp.dot(a_ref[...].astype(BF16), w_ref[...], preferred_element_type=F32)
    o_ref[...] = _layer_norm(alpha * x_ref[...] + h, g_ref[...], b_ref[...])


def _proj_ln(a, w, x2, ln_g, ln_b, alpha, tm=512):
    t, d = x2.shape
    tm = min(tm, t)
    row = lambda i: (i, 0)
    return pl.pallas_call(
        functools.partial(_proj_ln_kernel, alpha=alpha),
        grid=(t // tm,),
        in_specs=[pl.BlockSpec((tm, a.shape[1]), row), _const_spec(w.shape), pl.BlockSpec((tm, d), row),
                  _const_spec((1, d)), _const_spec((1, d))],
        out_specs=pl.BlockSpec((tm, d), row),
        out_shape=jax.ShapeDtypeStruct((t, d), F32),
        compiler_params=_params("arbitrary"),
        name="proj_ln",
    )(a, w.astype(BF16), x2, ln_g.reshape(1, d), ln_b.reshape(1, d))


def _ffn_ln_kernel(x_ref, w1_ref, w3_ref, w2_ref, g_ref, b_ref, o_ref, *, alpha, chunk_major):
    x = x_ref[...]
    xb = x.astype(BF16)
    hid = _silu(jnp.dot(xb, w1_ref[...], preferred_element_type=F32)) * jnp.dot(xb, w3_ref[...], preferred_element_type=F32)
    f = jnp.dot(hid.astype(BF16), w2_ref[...], preferred_element_type=F32)
    y = _layer_norm(alpha * x + f, g_ref[...], b_ref[...])
    if chunk_major:
        for c in range(o_ref.shape[1]):
            o_ref[:, c, :] = y[c * CHUNK:(c + 1) * CHUNK, :]
    else:
        o_ref[...] = y


def _ffn_ln(x2, w1, w3, w2, ln_g, ln_b, alpha, chunk_major=False, tm=512):
    t, d = x2.shape
    tm = min(tm, t)
    row = lambda i: (i, 0)
    if chunk_major:
        out_spec = pl.BlockSpec((CHUNK, tm // CHUNK, d), lambda i: (0, i, 0))
        out_shape = jax.ShapeDtypeStruct((CHUNK, t // CHUNK, d), F32)
    else:
        out_spec = pl.BlockSpec((tm, d), row)
        out_shape = jax.ShapeDtypeStruct((t, d), F32)
    return pl.pallas_call(
        functools.partial(_ffn_ln_kernel, alpha=alpha, chunk_major=chunk_major),
        grid=(t // tm,),
        in_specs=[pl.BlockSpec((tm, d), row), _const_spec(w1.shape), _const_spec(w3.shape), _const_spec(w2.shape),
                  _const_spec((1, d)), _const_spec((1, d))],
        out_specs=out_spec,
        out_shape=out_shape,
        compiler_params=_params("arbitrary"),
        name="ffn_ln",
    )(x2, w1.astype(BF16), w3.astype(BF16), w2.astype(BF16), ln_g.reshape(1, d), ln_b.reshape(1, d))


def _gdn_layer(x2, bsz, seq, w_in, conv_w, a_log, dt_bias, norm_g, w_out, ln_g, ln_b, alpha):
    q, k, v, z, beta, gc, gct = _gdn_in(x2, w_in, conv_w, a_log, dt_bias, seq)
    o = _gdn_core(q, k, v, z, beta, gc, gct, norm_g, bsz, seq)
    return _proj_ln(o, w_out, x2, ln_g, ln_b, alpha)


def _s5_taps_kernel(ce_ref, b_ref, o_ref):
    o_ref[0] = _dot_f32(ce_ref[0], b_ref[0])


def _s5_tables(lam_re, lam_im, log_dt, b_re, b_im, c_re, c_im, d_skip, n_per_seq):
    g, p = lam_re.shape
    gs = b_re.shape[-1]
    lr, li = lam_re.astype(F32), lam_im.astype(F32)
    dt = jnp.exp(log_dt.astype(F32))[:, None]
    zr, zi = lr * dt, li * dt

    def a_pow(steps):
        s = steps.astype(F32)[:, None, None]
        mag = jnp.exp(zr[None] * s)
        return mag * jnp.cos(zi[None] * s), mag * jnp.sin(zi[None] * s)

    ar, ai = a_pow(jnp.arange(CHUNK + 1))
    nr, ni = ar[1] - 1.0, ai[1]
    den = lr * lr + li * li
    qr, qi = (nr * lr + ni * li) / den, (ni * lr - nr * li) / den
    bbr = qr[..., None] * b_re - qi[..., None] * b_im
    bbi = qr[..., None] * b_im + qi[..., None] * b_re
    cr, ci = c_re.astype(F32), c_im.astype(F32)
    car = cr[:, None] * jnp.moveaxis(ar, 0, 1)[:, :, None, :] - ci[:, None] * jnp.moveaxis(ai, 0, 1)[:, :, None, :]
    cai = cr[:, None] * jnp.moveaxis(ai, 0, 1)[:, :, None, :] + ci[:, None] * jnp.moveaxis(ar, 0, 1)[:, :, None, :]
    ca = jnp.concatenate([car, -cai], axis=-1)
    ce = ca[:, :CHUNK].reshape(g, CHUNK * gs, 2 * p)
    bst = jnp.concatenate([bbr, bbi], axis=1)
    taps = pl.pallas_call(
        _s5_taps_kernel,
        grid=(g,),
        in_specs=[pl.BlockSpec((1, CHUNK * gs, 2 * p), lambda i: (i, 0, 0)),
                  pl.BlockSpec((1, 2 * p, gs), lambda i: (i, 0, 0))],
        out_specs=pl.BlockSpec((1, CHUNK * gs, gs), lambda i: (i, 0, 0)),
        out_shape=jax.ShapeDtypeStruct((g, CHUNK * gs, gs), F32),
        compiler_params=_params("arbitrary"),
        name="s5_taps",
    )(ce, bst).reshape(g, CHUNK, gs, gs)
    taps = taps.at[:, 0].add(d_skip.astype(F32).reshape(g, gs)[:, :, None] * jnp.eye(gs, dtype=F32))
    nb = CHUNK // S5_TBLK
    blk = jnp.arange(S5_TBLK)
    lag = S5_TBLK * jnp.arange(nb)[:, None, None] + blk[None, :, None] - blk[None, None, :]
    stk = jnp.where((lag >= 0)[None, :, :, :, None, None], taps[:, jnp.clip(lag, 0, CHUNK - 1)], 0.0)
    m_mat = jnp.transpose(stk, (0, 1, 2, 4, 3, 5)).reshape(g, CHUNK * gs, S5_TBLK * gs).astype(BF16)
    arr, aii = jnp.moveaxis(ar[:CHUNK][::-1], 0, 1), jnp.moveaxis(ai[:CHUNK][::-1], 0, 1)
    wr = arr[:, :, :, None] * bbr[:, None] - aii[:, :, :, None] * bbi[:, None]
    wi = arr[:, :, :, None] * bbi[:, None] + aii[:, :, :, None] * bbr[:, None]
    w_mat = jnp.concatenate([jnp.transpose(wr, (0, 2, 1, 3)), jnp.transpose(wi, (0, 2, 1, 3))], axis=1)
    w_mat = w_mat.reshape(g, 2 * p, CHUNK * gs).astype(BF16)
    v_mat = ca[:, 1:].reshape(g, CHUNK * gs, 2 * p).astype(BF16)
    n_steps = max(1, (n_per_seq - 1).bit_length())
    apr, api = a_pow(CHUNK * 2 ** jnp.arange(n_steps))
    apr = jnp.moveaxis(apr, 0, 1)[..., None]
    api = jnp.moveaxis(api, 0, 1)[..., None]
    return m_mat, w_mat, v_mat, apr, api


def _s5_in_kernel(x_ref, wT_ref, o_ref):
    o_ref[0] = _dot_nt(wT_ref[...], x_ref[0]).astype(o_ref.dtype)


def _s5_scan_kernel(u_ref, m_ref, w_ref, v_ref, apr_ref, api_ref, o_ref, *, n_per_seq):
    rows, gs, cols = u_ref.shape
    p = apr_ref.shape[2]
    u = u_ref[...].reshape(rows * gs, cols)
    hin = jnp.dot(w_ref[0], u, preferred_element_type=F32)
    yr, yi = hin[:p], hin[p:]
    pos = lax.broadcasted_iota(jnp.int32, (p, cols), 1) % n_per_seq

    def shifted(a, d):
        return jnp.where(pos >= d, pltpu.roll(a, d, axis=1), 0.0)

    d, kk = 1, 0
    while d < n_per_seq:
        sr, si = shifted(yr, d), shifted(yi, d)
        ar, ai = apr_ref[0, kk], api_ref[0, kk]
        yr, yi = yr + ar * sr - ai * si, yi + ar * si + ai * sr
        d, kk = 2 * d, kk + 1
    h_prev = jnp.concatenate([shifted(yr, 1), shifted(yi, 1)], axis=0).astype(BF16)
    nb = rows // S5_TBLK
    br = S5_TBLK * gs
    ys = [jnp.dot(v_ref[0, b * br:(b + 1) * br, :], h_prev, preferred_element_type=F32) for b in range(nb)]
    for sb in range(nb):
        part = jnp.dot(m_ref[0, 0:(nb - sb) * br, :], u[sb * br:(sb + 1) * br, :], preferred_element_type=F32)
        for j in range(nb - sb):
            ys[sb + j] = ys[sb + j] + part[j * br:(j + 1) * br, :]
    y = jnp.concatenate(ys, axis=0)
    o_ref[0] = _gelu(y).reshape(rows, gs, cols).astype(o_ref.dtype)


def _s5_out_kernel(h_ref, w_ref, x_ref, g_ref, b_ref, o_ref, *, alpha):
    d = x_ref.shape[-1]
    ht = h_ref[...].reshape(d, h_ref.shape[-1])
    vg = lax.dot_general(ht, w_ref[...], (((0,), (0,)), ((), ())), preferred_element_type=F32)
    o_ref[0] = _layer_norm(alpha * x_ref[0] + vg[:, :d] * jax.nn.sigmoid(vg[:, d:]), g_ref[...], b_ref[...])


def _s5_layer(xt, seq, w_in, lam_re, lam_im, log_dt, b_re, b_im, c_re, c_im, d_skip, w_glu, ln_g, ln_b, alpha):
    _, cols, d = xt.shape
    n_per_seq = seq // CHUNK
    g, p = lam_re.shape
    gs = d // g
    m_mat, w_mat, v_mat, apr, api = _s5_tables(lam_re, lam_im, log_dt, b_re, b_im, c_re, c_im, d_skip, n_per_seq)
    u_t = pl.pallas_call(
        _s5_in_kernel,
        grid=(CHUNK,),
        in_specs=[pl.BlockSpec((1, cols, d), lambda i: (i, 0, 0)), _const_spec((d, d))],
        out_specs=pl.BlockSpec((1, d, cols), lambda i: (i, 0, 0)),
        out_shape=jax.ShapeDtypeStruct((CHUNK, d, cols), BF16),
        compiler_params=_params("arbitrary"),
        name="s5_in",
    )(xt, w_in.T.astype(BF16))
    n_steps = apr.shape[1]
    hid = pl.pallas_call(
        functools.partial(_s5_scan_kernel, n_per_seq=n_per_seq),
        grid=(g,),
        in_specs=[pl.BlockSpec((CHUNK, gs, cols), lambda i: (0, i, 0)),
                  pl.BlockSpec((1, CHUNK * gs, S5_TBLK * gs), lambda i: (i, 0, 0)),
                  pl.BlockSpec((1, 2 * p, CHUNK * gs), lambda i: (i, 0, 0)),
                  pl.BlockSpec((1, CHUNK * gs, 2 * p), lambda i: (i, 0, 0)),
                  pl.BlockSpec((1, n_steps, p, 1), lambda i: (i, 0, 0, 0)),
                  pl.BlockSpec((1, n_steps, p, 1), lambda i: (i, 0, 0, 0))],
        out_specs=pl.BlockSpec((1, CHUNK, gs, cols), lambda i: (i, 0, 0, 0)),
        out_shape=jax.ShapeDtypeStruct((g, CHUNK, gs, cols), BF16),
        compiler_params=_params("arbitrary"),
        name="s5_scan",
    )(u_t, m_mat, w_mat, v_mat, apr, api)
    return pl.pallas_call(
        functools.partial(_s5_out_kernel, alpha=alpha),
        grid=(CHUNK,),
        in_specs=[pl.BlockSpec((g, 1, gs, cols), lambda i: (0, i, 0, 0)), _const_spec(w_glu.shape),
                  pl.BlockSpec((1, cols, d), lambda i: (i, 0, 0)), _const_spec((1, d)), _const_spec((1, d))],
        out_specs=pl.BlockSpec((1, cols, d), lambda i: (i, 0, 0)),
        out_shape=jax.ShapeDtypeStruct((CHUNK, cols, d), F32),
        compiler_params=_params("arbitrary"),
        name="s5_out",
    )(hid, w_glu.astype(BF16), xt, ln_g.reshape(1, d), ln_b.reshape(1, d))


def _top2_gates(logits):
    n_e = logits.shape[-1]
    lane = lax.broadcasted_iota(jnp.int32, logits.shape, 1)
    m1 = jnp.max(logits, axis=-1, keepdims=True)
    i1 = jnp.min(jnp.where(logits == m1, lane, n_e), axis=-1, keepdims=True)
    rest = jnp.where(lane == i1, -jnp.inf, logits)
    m2 = jnp.max(rest, axis=-1, keepdims=True)
    i2 = jnp.min(jnp.where(rest == m2, lane, n_e), axis=-1, keepdims=True)
    e2 = jnp.exp(m2 - m1)
    den = 1.0 + e2
    return i1, i2, 1.0 / den, e2 / den


def _moe_route_kernel(x_ref, wh_ref, wl_ref, br_ref, low_ref, idx_ref, gate_ref, cnt_ref, base_ref):
    @pl.when(pl.program_id(0) == 0)
    def _():
        base_ref[...] = jnp.zeros(base_ref.shape, F32)

    x = x_ref[...]
    xh = x.astype(BF16)
    xl = (x - xh.astype(F32)).astype(BF16)
    logits = (jnp.dot(xh, wh_ref[...], preferred_element_type=F32)
              + (jnp.dot(xh, wl_ref[...], preferred_element_type=F32) + jnp.dot(xl, wh_ref[...], preferred_element_type=F32)))
    i1, i2, g1, g2 = _top2_gates(logits + br_ref[...])
    lane = lax.broadcasted_iota(jnp.int32, (x_ref.shape[0], wh_ref.shape[1]), 1)
    sel = jnp.where((lane == i1) | (lane == i2), 1.0, 0.0)
    rank = base_ref[...] + jnp.dot(low_ref[...], sel.astype(BF16), preferred_element_type=F32)
    p1 = jnp.sum(jnp.where(lane == i1, rank, 0.0), axis=-1, keepdims=True).astype(jnp.int32)
    p2 = jnp.sum(jnp.where(lane == i2, rank, 0.0), axis=-1, keepdims=True).astype(jnp.int32)
    idx_ref[...] = jnp.where(lane == 0, i1, jnp.where(lane == 1, i2, jnp.where(lane == 2, p1, jnp.where(lane == 3, p2, 0))))
    gate_ref[...] = jnp.where(lane == 0, g1, jnp.where(lane == 1, g2, 0.0))
    base_ref[...] += jnp.sum(sel, axis=0, keepdims=True)
    cnt_ref[...] = base_ref[...].astype(jnp.int32)


def _row_gather_start(src_hbm, idx_ref, dst_ref, sem, n_rows):
    for r in range(n_rows):
        pltpu.make_async_copy(src_hbm.at[pl.ds(idx_ref[0, 0, r], 1)], dst_ref.at[pl.ds(r, 1)], sem).start()


def _row_gather_wait(src_hbm, dst_ref, sem, n_rows):
    pltpu.make_async_copy(src_hbm.at[pl.ds(0, n_rows)], dst_ref, sem).wait()


def _moe_dispatch_kernel(pad_start_ref, pad_len_ref, tail_ref, d1_ref, d2_ref, x_hbm, xs_hbm, xbuf, zbuf,
                         in_sem, sem, pad_sem):
    i = pl.program_id(0)
    n = pl.num_programs(0)
    n_buf, tb, _ = xbuf.shape

    def fetch(j, b):
        return pltpu.make_async_copy(x_hbm.at[pl.ds(pl.multiple_of(j * tb, tb), tb)], xbuf.at[b], in_sem.at[b])

    @pl.when(i == 0)
    def _():
        fetch(0, 0).start()
        zbuf[...] = jnp.zeros(zbuf.shape, F32)
        for e in range(pad_start_ref.shape[0]):
            def zero_row(k, carry, e=e):
                pltpu.make_async_copy(zbuf.at[pl.ds(0, 1)], xs_hbm.at[pl.ds(pad_start_ref[e] + k, 1)], pad_sem).start()
                return carry
            lax.fori_loop(0, pad_len_ref[e], zero_row, 0)

        def zero_piece(k, carry):
            row0 = pl.multiple_of(tail_ref[0] + k * tb, tb)
            pltpu.make_async_copy(zbuf, xs_hbm.at[pl.ds(row0, tb)], pad_sem).start()
            return carry
        lax.fori_loop(0, tail_ref[1], zero_piece, 0)
        for e in range(pad_start_ref.shape[0]):
            def wait_row(k, carry):
                pltpu.make_async_copy(zbuf.at[pl.ds(0, 1)], xs_hbm.at[pl.ds(0, 1)], pad_sem).wait()
                return carry
            lax.fori_loop(0, pad_len_ref[e], wait_row, 0)

        def wait_piece(k, carry):
            pltpu.make_async_copy(zbuf, xs_hbm.at[pl.ds(0, tb)], pad_sem).wait()
            return carry
        lax.fori_loop(0, tail_ref[1], wait_piece, 0)

    @pl.when(i + 1 < n)
    def _():
        fetch(i + 1, (i + 1) % n_buf).start()

    buf = i % n_buf
    fetch(i, buf).wait()
    slot = i % 2
    for r in range(tb):
        row = xbuf.at[buf, pl.ds(r, 1)]
        pltpu.make_async_copy(row, xs_hbm.at[pl.ds(d1_ref[0, 0, r], 1)], sem.at[slot]).start()
        pltpu.make_async_copy(row, xs_hbm.at[pl.ds(d2_ref[0, 0, r], 1)], sem.at[slot]).start()

    def wait_tile(s):
        for _ in range(2):
            pltpu.make_async_copy(xbuf.at[0], xs_hbm.at[pl.ds(0, tb)], sem.at[s]).wait()

    @pl.when(i > 0)
    def _():
        wait_tile(1 - slot)

    @pl.when(i == n - 1)
    def _():
        wait_tile(slot)


def _moe_expert_kernel(tile_e_ref, n_used_ref, xs_ref, w1_ref, w3_ref, w2_ref, y_ref, w1b_ref, w3b_ref, w2b_ref):
    i = pl.program_id(0)
    used = i < n_used_ref[0]

    @pl.when((i == 0) | (tile_e_ref[i] != tile_e_ref[jnp.maximum(i - 1, 0)]))
    def _():
        w1b_ref[...] = w1_ref[0].astype(BF16)
        w3b_ref[...] = w3_ref[0].astype(BF16)
        w2b_ref[...] = w2_ref[0].astype(BF16)

    @pl.when(used)
    def _():
        xb = xs_ref[...].astype(BF16)
        hid = _silu(jnp.dot(xb, w1b_ref[...], preferred_element_type=F32)) * jnp.dot(xb, w3b_ref[...], preferred_element_type=F32)
        y_ref[...] = jnp.dot(hid.astype(BF16), w2b_ref[...], preferred_element_type=F32)

    @pl.when(jnp.logical_not(used))
    def _():
        y_ref[...] = jnp.zeros(y_ref.shape, F32)


def _moe_combine_kernel(d1_ref, d2_ref, d1n_ref, d2n_ref, x_ref, gate_ref, y_hbm, g_ref, b_ref, o_ref, ybuf, sem,
                        *, alpha):
    i = pl.program_id(0)
    n = pl.num_programs(0)
    nf, cb, d = x_ref.shape
    tb = nf * cb
    slot = i % 2

    def start(a_ref, b_ref, s):
        _row_gather_start(y_hbm, a_ref, ybuf.at[s, 0], sem.at[s], tb)
        _row_gather_start(y_hbm, b_ref, ybuf.at[s, 1], sem.at[s], tb)

    @pl.when(i == 0)
    def _():
        start(d1_ref, d2_ref, 0)

    @pl.when(i + 1 < n)
    def _():
        start(d1n_ref, d2n_ref, 1 - slot)

    _row_gather_wait(y_hbm, ybuf.at[slot, 0], sem.at[slot], tb)
    _row_gather_wait(y_hbm, ybuf.at[slot, 1], sem.at[slot], tb)
    gate = gate_ref[...].reshape(tb, gate_ref.shape[-1])
    f = gate[:, 0:1] * ybuf[slot, 0] + gate[:, 1:2] * ybuf[slot, 1]
    y = _layer_norm(alpha * x_ref[...].reshape(tb, d) + f, g_ref[...], b_ref[...])
    for k in range(nf):
        o_ref[:, k, :] = y[k * cb:(k + 1) * cb, :]


def _moe_layer(xt, w_router, b_router, w1, w3, w2, ln_g, ln_b, alpha, tm=1024, tr=256, tb=256):
    n_f, n_c, d = xt.shape
    t = n_f * n_c
    x2 = xt.reshape(t, d)
    n_e, _, dff = w1.shape
    tm = min(tm, t)
    tb = min(tb, COMBINE_FRAMES * n_c)
    assert tr % tb == 0, "the grouped buffer's tail is zeroed in tb-row pieces"
    idx = jnp.arange(tm)
    low = (idx[:, None] > idx[None, :]).astype(BF16)
    wr_hi = w_router.astype(BF16)
    route_i, route_g, counts = pl.pallas_call(
        _moe_route_kernel,
        grid=(t // tm,),
        in_specs=[pl.BlockSpec((tm, d), lambda i: (i, 0)), _const_spec((d, n_e)), _const_spec((d, n_e)),
                  _const_spec((1, n_e)), _const_spec((tm, tm))],
        out_specs=[pl.BlockSpec((tm, n_e), lambda i: (i, 0)), pl.BlockSpec((tm, n_e), lambda i: (i, 0)),
                   pl.BlockSpec((1, n_e), lambda i: (0, 0))],
        out_shape=[jax.ShapeDtypeStruct((t, n_e), jnp.int32), jax.ShapeDtypeStruct((t, n_e), F32),
                   jax.ShapeDtypeStruct((1, n_e), jnp.int32)],
        scratch_shapes=[pltpu.VMEM((1, n_e), F32)],
        compiler_params=_params("arbitrary"),
        name="moe_route",
    )(x2, wr_hi, (w_router.astype(F32) - wr_hi.astype(F32)).astype(BF16), b_router.reshape(1, n_e).astype(F32), low)

    n_tiles = (2 * t) // tr + n_e
    cnt = counts[0]
    size = ((cnt + tr - 1) // tr) * tr
    ends = jnp.cumsum(size)
    offs = ends - size
    e1, e2, p1, p2 = route_i[:, 0], route_i[:, 1], route_i[:, 2], route_i[:, 3]
    dest1 = offs[e1] + p1
    dest2 = offs[e2] + p2
    tile_e = jnp.minimum(jnp.sum(jnp.arange(n_tiles, dtype=jnp.int32)[:, None] * tr >= ends[None, :], axis=1),
                         n_e - 1).astype(jnp.int32)
    n_used = (ends[n_e - 1:] // tr).astype(jnp.int32)

    smem_blk = lambda n: pl.BlockSpec((1, 1, n), lambda i, *_: (i, 0, 0), memory_space=pltpu.SMEM)
    smem_next = lambda n, last: pl.BlockSpec((1, 1, n), lambda i, *_: (jnp.minimum(i + 1, last), 0, 0),
                                             memory_space=pltpu.SMEM)
    any_spec = pl.BlockSpec(memory_space=pl.ANY)
    xs = pl.pallas_call(
        _moe_dispatch_kernel,
        grid_spec=pltpu.PrefetchScalarGridSpec(
            num_scalar_prefetch=3,
            grid=(t // tb,),
            in_specs=[smem_blk(tb), smem_blk(tb), any_spec],
            out_specs=any_spec,
            scratch_shapes=[pltpu.VMEM((DISPATCH_BUFFERS, tb, d), F32), pltpu.VMEM((tb, d), F32),
                            pltpu.SemaphoreType.DMA((DISPATCH_BUFFERS,)), pltpu.SemaphoreType.DMA((2,)),
                            pltpu.SemaphoreType.DMA(())],
        ),
        out_shape=jax.ShapeDtypeStruct((n_tiles * tr, d), F32),
        compiler_params=_params("arbitrary"),
        name="moe_dispatch",
    )((offs + cnt).astype(jnp.int32), (size - cnt).astype(jnp.int32),
      jnp.concatenate([ends[n_e - 1:], (n_tiles * tr - ends[n_e - 1:]) // tb]).astype(jnp.int32),
      dest1.reshape(t // tb, 1, tb), dest2.reshape(t // tb, 1, tb), x2)

    tile = lambda i, te, nu: (i, 0)
    wspec = lambda shape: pl.BlockSpec(shape, lambda i, te, nu: (te[i], 0, 0), pipeline_mode=pl.Buffered(1))
    ys = pl.pallas_call(
        _moe_expert_kernel,
        grid_spec=pltpu.PrefetchScalarGridSpec(
            num_scalar_prefetch=2,
            grid=(n_tiles,),
            in_specs=[pl.BlockSpec((tr, d), tile), wspec((1, d, dff)), wspec((1, d, dff)), wspec((1, dff, d))],
            out_specs=pl.BlockSpec((tr, d), tile),
            scratch_shapes=[pltpu.VMEM((d, dff), BF16), pltpu.VMEM((d, dff), BF16), pltpu.VMEM((dff, d), BF16)],
        ),
        out_shape=jax.ShapeDtypeStruct((n_tiles * tr, d), F32),
        compiler_params=_params("arbitrary"),
        name="moe_experts",
    )(tile_e, n_used, xs, w1.astype(F32), w3.astype(F32), w2.astype(F32))

    nf = COMBINE_FRAMES
    cb = tb // nf
    n_fb, n_cb = n_f // nf, n_c // cb
    n_tb = n_fb * n_cb
    tile_order = lambda a: a.reshape(n_fb, nf, n_cb, cb).transpose(0, 2, 1, 3).reshape(n_tb, 1, tb)
    d1, d2 = tile_order(dest1), tile_order(dest2)
    in_blk = lambda i: (i // n_cb, i % n_cb, 0)
    out = pl.pallas_call(
        functools.partial(_moe_combine_kernel, alpha=alpha),
        grid=(n_tb,),
        in_specs=[smem_blk(tb), smem_blk(tb), smem_next(tb, n_tb - 1), smem_next(tb, n_tb - 1),
                  pl.BlockSpec((nf, cb, d), in_blk), pl.BlockSpec((nf, cb, n_e), in_blk),
                  pl.BlockSpec(memory_space=pl.ANY), _const_spec((1, d)), _const_spec((1, d))],
        out_specs=pl.BlockSpec((cb, nf, d), lambda i: (i % n_cb, i // n_cb, 0)),
        out_shape=jax.ShapeDtypeStruct((n_c, n_f, d), F32),
        scratch_shapes=[pltpu.VMEM((2, 2, tb, d), F32), pltpu.SemaphoreType.DMA((2,))],
        compiler_params=_params("arbitrary"),
        name="moe_combine",
    )(d1, d2, d1, d2, xt, route_g.reshape(n_f, n_c, n_e), ys, ln_g.reshape(1, d), ln_b.reshape(1, d))
    return out.reshape(t, d)


def kernel(x, gdn_w_in, gdn_conv_w, gdn_a_log, gdn_dt_bias, gdn_norm_g, gdn_w_out, ffn_w1, ffn_w3, ffn_w2,
           s5_w_in, s5_lam_re, s5_lam_im, s5_log_dt, s5_b_re, s5_b_im, s5_c_re, s5_c_im, s5_d, s5_w_glu,
           moe_w_router, moe_b_router, moe_w1, moe_w3, moe_w2, ln_g, ln_b):
    bsz, seq, d = x.shape
    depth = ln_g.shape[0]
    alpha = (2 * depth) ** 0.25
    x2 = x.reshape(bsz * seq, d)
    for i in range(depth):
        j = i // 2
        if i % 2 == 0:
            x2 = _gdn_layer(x2, bsz, seq, gdn_w_in[j], gdn_conv_w[j], gdn_a_log[j], gdn_dt_bias[j], gdn_norm_g[j],
                            gdn_w_out[j], ln_g[i, 0], ln_b[i, 0], alpha)
            x2 = _ffn_ln(x2, ffn_w1[j], ffn_w3[j], ffn_w2[j], ln_g[i, 1], ln_b[i, 1], alpha,
                         chunk_major=i + 1 < depth)
        else:
            xt = _s5_layer(x2, seq, s5_w_in[j], s5_lam_re[j], s5_lam_im[j], s5_log_dt[j], s5_b_re[j], s5_b_im[j],
                           s5_c_re[j], s5_c_im[j], s5_d[j], s5_w_glu[j], ln_g[i, 0], ln_b[i, 0], alpha)
            x2 = _moe_layer(xt, moe_w_router[j], moe_b_router[j], moe_w1[j], moe_w3[j], moe_w2[j],
                            ln_g[i, 1], ln_b[i, 1], alpha)
    return x2.reshape(bsz, seq, d)
```

```python
import functools

import jax
import jax.numpy as jnp
from jax import lax
from jax.experimental import pallas as pl
from jax.experimental.pallas import tpu as pltpu

F32 = jnp.float32
BF16 = jnp.bfloat16

CHUNK = 64
GDN_HEADS = 8
GDN_DK = 128
GDN_CONV = 4
S5_GROUP = 16
S5_TBLK = 16
COMBINE_FRAMES = 8
DISPATCH_BUFFERS = 3
DMA_PRIORITIES = 2
LN_EPS = 1e-5
NORM_EPS = 1e-6

VMEM_LIMIT_BYTES = 56 * 1024 * 1024
CONV_CARRY_ROWS = 8
MXU_COLS = 256
PREP_CHUNKS = 4


def _params(*semantics):
    return pltpu.CompilerParams(dimension_semantics=semantics, vmem_limit_bytes=VMEM_LIMIT_BYTES)


def _const_spec(shape):
    nd = len(shape)
    return pl.BlockSpec(shape, lambda *_: (0,) * nd, pipeline_mode=pl.Buffered(1))


def _dot(a, b):
    return jnp.dot(a.astype(BF16), b.astype(BF16), preferred_element_type=F32)


def _dot_nt(a, b):
    return lax.dot_general(a.astype(BF16), b.astype(BF16), (((1,), (1,)), ((), ())),
                           preferred_element_type=F32)


def _dot_f32(a, b):
    return jnp.dot(a, b, preferred_element_type=F32, precision=lax.Precision.HIGHEST)


def _silu(x):
    return x * jax.nn.sigmoid(x)


def _gelu(x):
    return 0.5 * x * (1.0 + lax.erf(x * (2.0 ** -0.5)))


def _softplus(x):
    return jnp.maximum(x, 0.0) + jnp.log1p(jnp.exp(-jnp.abs(x)))


def _layer_norm(y, g, b):
    mu = jnp.mean(y, axis=-1, keepdims=True)
    yc = y - mu
    var = jnp.mean(yc * yc, axis=-1, keepdims=True)
    return yc * lax.rsqrt(var + LN_EPS) * g + b


def _gdn_in_kernel(x_ref, wqkv_ref, wz_ref, wb_ref, wa_ref, waT_ref, conv_ref, alog_ref, dtb_ref,
                   alogc_ref, dtbc_ref, tri_ref, triT_ref,
                   q_ref, k_ref, v_ref, z_ref, beta_ref, gc_ref, gcT_ref, carry_ref,
                   *, tiles_per_seq, tm, n_qk):
    i = pl.program_id(0)

    @pl.when(i % tiles_per_seq == 0)
    def _():
        carry_ref[...] = jnp.zeros(carry_ref.shape, F32)

    xb = x_ref[...].astype(BF16)
    z_ref[...] = jnp.dot(xb, wz_ref[...], preferred_element_type=F32).astype(z_ref.dtype)
    beta_ref[...] = jax.nn.sigmoid(jnp.dot(xb, wb_ref[...], preferred_element_type=F32))

    g_col = -jnp.exp(alog_ref[...]) * _softplus(jnp.dot(xb, wa_ref[...], preferred_element_type=F32) + dtb_ref[...])
    for c in range(tm // CHUNK):
        rows = slice(c * CHUNK, (c + 1) * CHUNK)
        gc_ref[rows, :] = _dot_f32(tri_ref[...], g_col[rows, :])
        a_row = lax.dot_general(waT_ref[...], xb[rows, :], (((1,), (1,)), ((), ())), preferred_element_type=F32)
        g_row = -jnp.exp(alogc_ref[...]) * _softplus(a_row + dtbc_ref[...])
        gcT_ref[c] = _dot_f32(g_row, triT_ref[...])

    for blk in range(carry_ref.shape[1] // MXU_COLS):
        cols = slice(blk * MXU_COLS, (blk + 1) * MXU_COLS)
        p = jnp.dot(xb, wqkv_ref[:, cols], preferred_element_type=F32)
        ext = jnp.concatenate([carry_ref[:, cols], p], axis=0)
        carry_ref[:, cols] = p[tm - CONV_CARRY_ROWS:tm, :]
        acc = conv_ref[GDN_CONV - 1:GDN_CONV, cols] * p
        for j in range(1, GDN_CONV):
            shifted = pltpu.roll(ext, j, axis=0)[CONV_CARRY_ROWS:, :]
            acc = acc + conv_ref[GDN_CONV - 1 - j:GDN_CONV - j, cols] * shifted
        y = _silu(acc)
        for hh in range(MXU_COLS // GDN_DK):
            head = blk * (MXU_COLS // GDN_DK) + hh
            yh = y[:, hh * GDN_DK:(hh + 1) * GDN_DK]
            if head < 2 * n_qk:
                yh = yh * lax.rsqrt(jnp.sum(yh * yh, axis=-1, keepdims=True) + NORM_EPS)
            if head < n_qk:
                q_ref[:, head * GDN_DK:(head + 1) * GDN_DK] = (yh * (GDN_DK ** -0.5)).astype(q_ref.dtype)
            elif head < 2 * n_qk:
                k_ref[:, (head - n_qk) * GDN_DK:(head - n_qk + 1) * GDN_DK] = yh.astype(k_ref.dtype)
            else:
                v_ref[:, (head - 2 * n_qk) * GDN_DK:(head - 2 * n_qk + 1) * GDN_DK] = yh.astype(v_ref.dtype)


def _gdn_in(x2, w_in, conv_w, a_log, dt_bias, seq, tm=256):
    t, d = x2.shape
    tm = min(tm, seq)
    nh = GDN_HEADS
    qk = nh * GDN_DK
    dv = (w_in.shape[1] - 2 * qk - 2 * nh) // 2
    wqkv = w_in[:, :2 * qk + dv].astype(BF16)
    wz = w_in[:, 2 * qk + dv:2 * qk + 2 * dv].astype(BF16)
    wb = w_in[:, 2 * qk + 2 * dv:2 * qk + 2 * dv + nh].astype(BF16)
    wa = w_in[:, 2 * qk + 2 * dv + nh:].astype(BF16)
    idx = jnp.arange(CHUNK)
    tri = (idx[:, None] >= idx[None, :]).astype(F32)
    n_chunks = t // CHUNK
    kern = functools.partial(_gdn_in_kernel, tiles_per_seq=seq // tm, tm=tm, n_qk=nh)
    row = lambda i: (i, 0)
    return pl.pallas_call(
        kern,
        grid=(t // tm,),
        in_specs=[
            pl.BlockSpec((tm, d), row),
            _const_spec(wqkv.shape), _const_spec(wz.shape), _const_spec(wb.shape), _const_spec(wa.shape),
            _const_spec((nh, d)), _const_spec(conv_w.shape),
            _const_spec((1, nh)), _const_spec((1, nh)), _const_spec((nh, 1)), _const_spec((nh, 1)),
            _const_spec((CHUNK, CHUNK)), _const_spec((CHUNK, CHUNK)),
        ],
        out_specs=[
            pl.BlockSpec((tm, qk), row), pl.BlockSpec((tm, qk), row), pl.BlockSpec((tm, dv), row),
            pl.BlockSpec((tm, dv), row), pl.BlockSpec((tm, nh), row), pl.BlockSpec((tm, nh), row),
            pl.BlockSpec((tm // CHUNK, nh, CHUNK), lambda i: (i, 0, 0)),
        ],
        out_shape=[
            jax.ShapeDtypeStruct((t, qk), BF16), jax.ShapeDtypeStruct((t, qk), BF16),
            jax.ShapeDtypeStruct((t, dv), BF16), jax.ShapeDtypeStruct((t, dv), BF16),
            jax.ShapeDtypeStruct((t, nh), F32), jax.ShapeDtypeStruct((t, nh), F32),
            jax.ShapeDtypeStruct((n_chunks, nh, CHUNK), F32),
        ],
        scratch_shapes=[pltpu.VMEM((CONV_CARRY_ROWS, 2 * qk + dv), F32)],
        compiler_params=_params("arbitrary"),
        name="gdn_in",
    )(x2, wqkv, wz, wb, wa, wa.T, conv_w.astype(F32),
      a_log.reshape(1, nh).astype(F32), dt_bias.reshape(1, nh).astype(F32),
      a_log.reshape(nh, 1).astype(F32), dt_bias.reshape(nh, 1).astype(F32), tri, tri.T)


def _gdn_core_kernel(q_ref, k_ref, v_ref, z_ref, beta_ref, gc_ref, gcT_ref, ng_ref, o_ref,
                     s_ref, u_ref, wq_ref, kd_ref, a_ref, *, n_sub):
    bsz, n_pair = s_ref.shape[0], s_ref.shape[1]
    pw = 2 * GDN_DK
    cc = CHUNK

    @pl.when(pl.program_id(0) == 0)
    def _():
        s_ref[...] = jnp.zeros(s_ref.shape, F32)

    r = lax.broadcasted_iota(jnp.int32, (cc, 2 * cc), 0)
    lane_m = lax.broadcasted_iota(jnp.int32, (cc, 2 * cc), 1)
    c = lane_m & (cc - 1)
    causal = r >= c
    strict = r > c
    eye = jnp.where(r == c, 1.0, 0.0)
    first_m = lane_m < cc
    first_w = lax.broadcasted_iota(jnp.int32, (cc, pw), 1) < GDN_DK
    sr = lax.broadcasted_iota(jnp.int32, (pw, pw), 0) < GDN_DK
    sc = lax.broadcasted_iota(jnp.int32, (pw, pw), 1) < GDN_DK
    same_head = sr == sc
    first_row = lax.broadcasted_iota(jnp.int32, (pw, 1), 0) < GDN_DK

    def block_diag(x, first):
        return jnp.concatenate([jnp.where(first, x, 0.0), jnp.where(first, 0.0, x)], axis=0)

    def pair_cols(col8, p, first):
        return jnp.where(first, col8[:, 2 * p:2 * p + 1], col8[:, 2 * p + 1:2 * p + 2])

    def prep(it, carry):
        chains = []
        for j in range(PREP_CHUNKS):
            idx = it * PREP_CHUNKS + j
            b, ci = idx // n_sub, idx % n_sub
            rows = pl.ds(pl.multiple_of(ci * cc, cc), cc)
            beta, gc, gct = beta_ref[b, rows, :], gc_ref[b, rows, :], gcT_ref[b, ci]
            for p in range(n_pair):
                chains.append((b, ci, rows, p, beta, gc, gct))
        ch = range(len(chains))
        lanes = [slice(p * pw, (p + 1) * pw) for (_, _, _, p, _, _, _) in chains]
        q = [q_ref[b, rows, lanes[i]].astype(F32) for i, (b, _, rows, *_) in enumerate(chains)]
        k = [k_ref[b, rows, lanes[i]].astype(F32) for i, (b, _, rows, *_) in enumerate(chains)]
        v = [v_ref[b, rows, lanes[i]].astype(F32) for i, (b, _, rows, *_) in enumerate(chains)]
        g_m = [pair_cols(gc, p, first_m) for (_, _, _, p, _, gc, _) in chains]
        g_w = [pair_cols(gc, p, first_w) for (_, _, _, p, _, gc, _) in chains]
        b_w = [pair_cols(beta, p, first_w) for (_, _, _, p, beta, _, _) in chains]
        decay = [jnp.exp(jnp.where(causal, g_m[i] - chains[i][6][chains[i][3]:chains[i][3] + 1, :], -jnp.inf)) for i in ch]
        eg = [jnp.exp(g_w[i]) for i in ch]
        kb = [k[i] * b_w[i] for i in ch]
        k_bd = [block_diag(k[i], first_w) for i in ch]
        m = [jnp.where(strict, _dot_nt(kb[i], k_bd[i]) * decay[i], 0.0) for i in ch]
        qk = [_dot_nt(q[i], k_bd[i]) for i in ch]
        t_inv = [eye - m[i] for i in ch]
        step = 1
        while 2 * step <= cc // 2:
            m = [_dot(m[i], block_diag(m[i], first_m)) for i in ch]
            t_inv = [t_inv[i] + _dot(t_inv[i], block_diag(m[i], first_m)) for i in ch]
            step *= 2
        rhs = [jnp.concatenate([block_diag(v[i] * b_w[i], first_w), block_diag(kb[i] * eg[i], first_w)], axis=1)
               for i in ch]
        uw = [_dot(t_inv[i], rhs[i]) for i in ch]
        for i, (b, ci, _, p, _, _, _) in enumerate(chains):
            u_ref[b, ci, p] = uw[i][:, :pw]
            wq_ref[b, ci, p, 0:cc, :] = uw[i][:, pw:].astype(BF16)
            wq_ref[b, ci, p, cc:2 * cc, :] = (q[i] * eg[i]).astype(BF16)
            kd_ref[b, ci, p] = (k[i] * jnp.exp(g_w[i][cc - 1:cc, :] - g_w[i])).T.astype(BF16)
            a_ref[b, ci, p] = (qk[i] * decay[i]).astype(BF16)
        return carry

    lax.fori_loop(0, bsz * n_sub // PREP_CHUNKS, prep, 0)

    chains = [(b, p) for b in range(bsz) for p in range(n_pair)]
    ch = range(len(chains))

    def scan(ci, carry):
        rows = pl.ds(pl.multiple_of(ci * cc, cc), cc)
        e_last = [jnp.exp(gc_ref[b, pl.ds(ci * cc + cc - 1, 1), :]) for b in range(bsz)]
        s = [s_ref[b, p] for (b, p) in chains]
        ws = [jnp.dot(wq_ref[b, ci, p], s[i].astype(BF16), preferred_element_type=F32) for i, (b, p) in enumerate(chains)]
        v_new = [u_ref[b, ci, p] - ws[i][:cc] for i, (b, p) in enumerate(chains)]
        o = [ws[i][cc:] + jnp.dot(a_ref[b, ci, p], block_diag(v_new[i], first_w).astype(BF16),
                                  preferred_element_type=F32) for i, (b, p) in enumerate(chains)]
        upd = [_dot(kd_ref[b, ci, p], v_new[i]) for i, (b, p) in enumerate(chains)]
        for i, (b, p) in enumerate(chains):
            e_col = jnp.where(first_row, e_last[b][:, 2 * p:2 * p + 1], e_last[b][:, 2 * p + 1:2 * p + 2])
            s_ref[b, p] = s[i] * e_col + jnp.where(same_head, upd[i], 0.0)
            for hh in range(2):
                cols = slice((2 * p + hh) * GDN_DK, (2 * p + hh + 1) * GDN_DK)
                oh = o[i][:, hh * GDN_DK:(hh + 1) * GDN_DK]
                on = oh * lax.rsqrt(jnp.mean(oh * oh, axis=-1, keepdims=True) + NORM_EPS) * ng_ref[...]
                o_ref[b, rows, cols] = (on * _silu(z_ref[b, rows, cols].astype(F32))).astype(o_ref.dtype)
        return carry

    lax.fori_loop(0, n_sub, scan, 0)


def _gdn_core(q, k, v, z, beta, gc, gct, norm_g, bsz, seq, tc=256):
    t, qk = q.shape
    tc = min(tc, seq)
    nh = GDN_HEADS
    dv = v.shape[1]
    n_sub = tc // CHUNK
    n_pair = nh // 2
    pw = 2 * GDN_DK
    blk3 = lambda w: pl.BlockSpec((bsz, tc, w), lambda i: (0, i, 0))
    seq3 = lambda a: a.reshape(bsz, seq, a.shape[-1])
    o = pl.pallas_call(
        functools.partial(_gdn_core_kernel, n_sub=n_sub),
        grid=(seq // tc,),
        in_specs=[blk3(qk), blk3(qk), blk3(dv), blk3(dv), blk3(nh), blk3(nh),
                  pl.BlockSpec((bsz, n_sub, n_pair, 2 * CHUNK), lambda i: (0, i, 0, 0)),
                  _const_spec((1, dv // nh))],
        out_specs=blk3(dv),
        out_shape=jax.ShapeDtypeStruct((bsz, seq, dv), BF16),
        scratch_shapes=[pltpu.VMEM((bsz, n_pair, pw, pw), F32),
                        pltpu.VMEM((bsz, n_sub, n_pair, CHUNK, pw), F32),
                        pltpu.VMEM((bsz, n_sub, n_pair, 2 * CHUNK, pw), BF16),
                        pltpu.VMEM((bsz, n_sub, n_pair, pw, CHUNK), BF16),
                        pltpu.VMEM((bsz, n_sub, n_pair, CHUNK, 2 * CHUNK), BF16)],
        compiler_params=_params("arbitrary"),
        name="gdn_core",
    )(seq3(q), seq3(k), seq3(v), seq3(z), seq3(beta), seq3(gc),
      gct.reshape(bsz, seq // CHUNK, n_pair, 2 * CHUNK), norm_g.reshape(1, -1).astype(F32))
    return o.reshape(t, dv)


def _proj_ln_kernel(a_ref, w_ref, x_ref, g_ref, b_ref, o_ref, *, alpha):
    h = jnp.dot(a_ref[...].astype(BF16), w_ref[...], preferred_element_type=F32)
    o_ref[...] = _layer_norm(alpha * x_ref[...] + h, g_ref[...], b_ref[...])


def _proj_ln(a, w, x2, ln_g, ln_b, alpha, tm=512):
    t, d = x2.shape
    tm = min(tm, t)
    row = lambda i: (i, 0)
    return pl.pallas_call(
        functools.partial(_proj_ln_kernel, alpha=alpha),
        grid=(t // tm,),
        in_specs=[pl.BlockSpec((tm, a.shape[1]), row), _const_spec(w.shape), pl.BlockSpec((tm, d), row),
                  _const_spec((1, d)), _const_spec((1, d))],
        out_specs=pl.BlockSpec((tm, d), row),
        out_shape=jax.ShapeDtypeStruct((t, d), F32),
        compiler_params=_params("arbitrary"),
        name="proj_ln",
    )(a, w.astype(BF16), x2, ln_g.reshape(1, d), ln_b.reshape(1, d))


def _ffn_ln_kernel(x_ref, w1_ref, w3_ref, w2_ref, g_ref, b_ref, o_ref, *, alpha, chunk_major):
    x = x_ref[...]
    xb = x.astype(BF16)
    hid = _silu(jnp.dot(xb, w1_ref[...], preferred_element_type=F32)) * jnp.dot(xb, w3_ref[...], preferred_element_type=F32)
    f = jnp.dot(hid.astype(BF16), w2_ref[...], preferred_element_type=F32)
    y = _layer_norm(alpha * x + f, g_ref[...], b_ref[...])
    if chunk_major:
        for c in range(o_ref.shape[1]):
            o_ref[:, c, :] = y[c * CHUNK:(c + 1) * CHUNK, :]
    else:
        o_ref[...] = y


def _ffn_ln(x2, w1, w3, w2, ln_g, ln_b, alpha, chunk_major=False, tm=512):
    t, d = x2.shape
    tm = min(tm, t)
    row = lambda i: (i, 0)
    if chunk_major:
        out_spec = pl.BlockSpec((CHUNK, tm // CHUNK, d), lambda i: (0, i, 0))
        out_shape = jax.ShapeDtypeStruct((CHUNK, t // CHUNK, d), F32)
    else:
        out_spec = pl.BlockSpec((tm, d), row)
        out_shape = jax.ShapeDtypeStruct((t, d), F32)
    return pl.pallas_call(
        functools.partial(_ffn_ln_kernel, alpha=alpha, chunk_major=chunk_major),
        grid=(t // tm,),
        in_specs=[pl.BlockSpec((tm, d), row), _const_spec(w1.shape), _const_spec(w3.shape), _const_spec(w2.shape),
                  _const_spec((1, d)), _const_spec((1, d))],
        out_specs=out_spec,
        out_shape=out_shape,
        compiler_params=_params("arbitrary"),
        name="ffn_ln",
    )(x2, w1.astype(BF16), w3.astype(BF16), w2.astype(BF16), ln_g.reshape(1, d), ln_b.reshape(1, d))


def _gdn_layer(x2, bsz, seq, w_in, conv_w, a_log, dt_bias, norm_g, w_out, ln_g, ln_b, alpha):
    q, k, v, z, beta, gc, gct = _gdn_in(x2, w_in, conv_w, a_log, dt_bias, seq)
    o = _gdn_core(q, k, v, z, beta, gc, gct, norm_g, bsz, seq)
    return _proj_ln(o, w_out, x2, ln_g, ln_b, alpha)


def _s5_taps_kernel(ce_ref, b_ref, o_ref):
    o_ref[0] = _dot_f32(ce_ref[0], b_ref[0])


def _s5_tables(lam_re, lam_im, log_dt, b_re, b_im, c_re, c_im, d_skip, n_per_seq):
    g, p = lam_re.shape
    gs = b_re.shape[-1]
    lr, li = lam_re.astype(F32), lam_im.astype(F32)
    dt = jnp.exp(log_dt.astype(F32))[:, None]
    zr, zi = lr * dt, li * dt

    def a_pow(steps):
        s = steps.astype(F32)[:, None, None]
        mag = jnp.exp(zr[None] * s)
        return mag * jnp.cos(zi[None] * s), mag * jnp.sin(zi[None] * s)

    ar, ai = a_pow(jnp.arange(CHUNK + 1))
    nr, ni = ar[1] - 1.0, ai[1]
    den = lr * lr + li * li
    qr, qi = (nr * lr + ni * li) / den, (ni * lr - nr * li) / den
    bbr = qr[..., None] * b_re - qi[..., None] * b_im
    bbi = qr[..., None] * b_im + qi[..., None] * b_re
    cr, ci = c_re.astype(F32), c_im.astype(F32)
    car = cr[:, None] * jnp.moveaxis(ar, 0, 1)[:, :, None, :] - ci[:, None] * jnp.moveaxis(ai, 0, 1)[:, :, None, :]
    cai = cr[:, None] * jnp.moveaxis(ai, 0, 1)[:, :, None, :] + ci[:, None] * jnp.moveaxis(ar, 0, 1)[:, :, None, :]
    ca = jnp.concatenate([car, -cai], axis=-1)
    ce = ca[:, :CHUNK].reshape(g, CHUNK * gs, 2 * p)
    bst = jnp.concatenate([bbr, bbi], axis=1)
    taps = pl.pallas_call(
        _s5_taps_kernel,
        grid=(g,),
        in_specs=[pl.BlockSpec((1, CHUNK * gs, 2 * p), lambda i: (i, 0, 0)),
                  pl.BlockSpec((1, 2 * p, gs), lambda i: (i, 0, 0))],
        out_specs=pl.BlockSpec((1, CHUNK * gs, gs), lambda i: (i, 0, 0)),
        out_shape=jax.ShapeDtypeStruct((g, CHUNK * gs, gs), F32),
        compiler_params=_params("arbitrary"),
        name="s5_taps",
    )(ce, bst).reshape(g, CHUNK, gs, gs)
    taps = taps.at[:, 0].add(d_skip.astype(F32).reshape(g, gs)[:, :, None] * jnp.eye(gs, dtype=F32))
    nb = CHUNK // S5_TBLK
    blk = jnp.arange(S5_TBLK)
    lag = S5_TBLK * jnp.arange(nb)[:, None, None] + blk[None, :, None] - blk[None, None, :]
    stk = jnp.where((lag >= 0)[None, :, :, :, None, None], taps[:, jnp.clip(lag, 0, CHUNK - 1)], 0.0)
    m_mat = jnp.transpose(stk, (0, 1, 2, 4, 3, 5)).reshape(g, CHUNK * gs, S5_TBLK * gs).astype(BF16)
    arr, aii = jnp.moveaxis(ar[:CHUNK][::-1], 0, 1), jnp.moveaxis(ai[:CHUNK][::-1], 0, 1)
    wr = arr[:, :, :, None] * bbr[:, None] - aii[:, :, :, None] * bbi[:, None]
    wi = arr[:, :, :, None] * bbi[:, None] + aii[:, :, :, None] * bbr[:, None]
    w_mat = jnp.concatenate([jnp.transpose(wr, (0, 2, 1, 3)), jnp.transpose(wi, (0, 2, 1, 3))], axis=1)
    w_mat = w_mat.reshape(g, 2 * p, CHUNK * gs).astype(BF16)
    v_mat = ca[:, 1:].reshape(g, CHUNK * gs, 2 * p).astype(BF16)
    n_steps = max(1, (n_per_seq - 1).bit_length())
    apr, api = a_pow(CHUNK * 2 ** jnp.arange(n_steps))
    apr = jnp.moveaxis(apr, 0, 1)[..., None]
    api = jnp.moveaxis(api, 0, 1)[..., None]
    return m_mat, w_mat, v_mat, apr, api


def _s5_in_kernel(x_ref, wT_ref, o_ref):
    o_ref[0] = _dot_nt(wT_ref[...], x_ref[0]).astype(o_ref.dtype)


def _s5_scan_kernel(u_ref, m_ref, w_ref, v_ref, apr_ref, api_ref, o_ref, *, n_per_seq):
    rows, gs, cols = u_ref.shape
    p = apr_ref.shape[2]
    u = u_ref[...].reshape(rows * gs, cols)
    hin = jnp.dot(w_ref[0], u, preferred_element_type=F32)
    yr, yi = hin[:p], hin[p:]
    pos = lax.broadcasted_iota(jnp.int32, (p, cols), 1) % n_per_seq

    def shifted(a, d):
        return jnp.where(pos >= d, pltpu.roll(a, d, axis=1), 0.0)

    d, kk = 1, 0
    while d < n_per_seq:
        sr, si = shifted(yr, d), shifted(yi, d)
        ar, ai = apr_ref[0, kk], api_ref[0, kk]
        yr, yi = yr + ar * sr - ai * si, yi + ar * si + ai * sr
        d, kk = 2 * d, kk + 1
    h_prev = jnp.concatenate([shifted(yr, 1), shifted(yi, 1)], axis=0).astype(BF16)
    nb = rows // S5_TBLK
    br = S5_TBLK * gs
    ys = [jnp.dot(v_ref[0, b * br:(b + 1) * br, :], h_prev, preferred_element_type=F32) for b in range(nb)]
    for sb in range(nb):
        part = jnp.dot(m_ref[0, 0:(nb - sb) * br, :], u[sb * br:(sb + 1) * br, :], preferred_element_type=F32)
        for j in range(nb - sb):
            ys[sb + j] = ys[sb + j] + part[j * br:(j + 1) * br, :]
    y = jnp.concatenate(ys, axis=0)
    o_ref[0] = _gelu(y).reshape(rows, gs, cols).astype(o_ref.dtype)


def _s5_out_kernel(h_ref, w_ref, x_ref, g_ref, b_ref, o_ref, *, alpha):
    d = x_ref.shape[-1]
    ht = h_ref[...].reshape(d, h_ref.shape[-1])
    vg = lax.dot_general(ht, w_ref[...], (((0,), (0,)), ((), ())), preferred_element_type=F32)
    o_ref[0] = _layer_norm(alpha * x_ref[0] + vg[:, :d] * jax.nn.sigmoid(vg[:, d:]), g_ref[...], b_ref[...])


def _s5_layer(xt, seq, w_in, lam_re, lam_im, log_dt, b_re, b_im, c_re, c_im, d_skip, w_glu, ln_g, ln_b, alpha):
    _, cols, d = xt.shape
    n_per_seq = seq // CHUNK
    g, p = lam_re.shape
    gs = d // g
    m_mat, w_mat, v_mat, apr, api = _s5_tables(lam_re, lam_im, log_dt, b_re, b_im, c_re, c_im, d_skip, n_per_seq)
    u_t = pl.pallas_call(
        _s5_in_kernel,
        grid=(CHUNK,),
        in_specs=[pl.BlockSpec((1, cols, d), lambda i: (i, 0, 0)), _const_spec((d, d))],
        out_specs=pl.BlockSpec((1, d, cols), lambda i: (i, 0, 0)),
        out_shape=jax.ShapeDtypeStruct((CHUNK, d, cols), BF16),
        compiler_params=_params("arbitrary"),
        name="s5_in",
    )(xt, w_in.T.astype(BF16))
    n_steps = apr.shape[1]
    hid = pl.pallas_call(
        functools.partial(_s5_scan_kernel, n_per_seq=n_per_seq),
        grid=(g,),
        in_specs=[pl.BlockSpec((CHUNK, gs, cols), lambda i: (0, i, 0)),
                  pl.BlockSpec((1, CHUNK * gs, S5_TBLK * gs), lambda i: (i, 0, 0)),
                  pl.BlockSpec((1, 2 * p, CHUNK * gs), lambda i: (i, 0, 0)),
                  pl.BlockSpec((1, CHUNK * gs, 2 * p), lambda i: (i, 0, 0)),
                  pl.BlockSpec((1, n_steps, p, 1), lambda i: (i, 0, 0, 0)),
                  pl.BlockSpec((1, n_steps, p, 1), lambda i: (i, 0, 0, 0))],
        out_specs=pl.BlockSpec((1, CHUNK, gs, cols), lambda i: (i, 0, 0, 0)),
        out_shape=jax.ShapeDtypeStruct((g, CHUNK, gs, cols), BF16),
        compiler_params=_params("arbitrary"),
        name="s5_scan",
    )(u_t, m_mat, w_mat, v_mat, apr, api)
    return pl.pallas_call(
        functools.partial(_s5_out_kernel, alpha=alpha),
        grid=(CHUNK,),
        in_specs=[pl.BlockSpec((g, 1, gs, cols), lambda i: (0, i, 0, 0)), _const_spec(w_glu.shape),
                  pl.BlockSpec((1, cols, d), lambda i: (i, 0, 0)), _const_spec((1, d)), _const_spec((1, d))],
        out_specs=pl.BlockSpec((1, cols, d), lambda i: (i, 0, 0)),
        out_shape=jax.ShapeDtypeStruct((CHUNK, cols, d), F32),
        compiler_params=_params("arbitrary"),
        name="s5_out",
    )(hid, w_glu.astype(BF16), xt, ln_g.reshape(1, d), ln_b.reshape(1, d))


def _top2_gates(logits):
    n_e = logits.shape[-1]
    lane = lax.broadcasted_iota(jnp.int32, logits.shape, 1)
    m1 = jnp.max(logits, axis=-1, keepdims=True)
    i1 = jnp.min(jnp.where(logits == m1, lane, n_e), axis=-1, keepdims=True)
    rest = jnp.where(lane == i1, -jnp.inf, logits)
    m2 = jnp.max(rest, axis=-1, keepdims=True)
    i2 = jnp.min(jnp.where(rest == m2, lane, n_e), axis=-1, keepdims=True)
    e2 = jnp.exp(m2 - m1)
    den = 1.0 + e2
    return i1, i2, 1.0 / den, e2 / den


def _moe_route_kernel(x_ref, wh_ref, wl_ref, br_ref, low_ref, idx_ref, gate_ref, cnt_ref, base_ref):
    @pl.when(pl.program_id(0) == 0)
    def _():
        base_ref[...] = jnp.zeros(base_ref.shape, F32)

    x = x_ref[...]
    xh = x.astype(BF16)
    xl = (x - xh.astype(F32)).astype(BF16)
    logits = (jnp.dot(xh, wh_ref[...], preferred_element_type=F32)
              + (jnp.dot(xh, wl_ref[...], preferred_element_type=F32) + jnp.dot(xl, wh_ref[...], preferred_element_type=F32)))
    i1, i2, g1, g2 = _top2_gates(logits + br_ref[...])
    lane = lax.broadcasted_iota(jnp.int32, (x_ref.shape[0], wh_ref.shape[1]), 1)
    sel = jnp.where((lane == i1) | (lane == i2), 1.0, 0.0)
    rank = base_ref[...] + jnp.dot(low_ref[...], sel.astype(BF16), preferred_element_type=F32)
    p1 = jnp.sum(jnp.where(lane == i1, rank, 0.0), axis=-1, keepdims=True).astype(jnp.int32)
    p2 = jnp.sum(jnp.where(lane == i2, rank, 0.0), axis=-1, keepdims=True).astype(jnp.int32)
    idx_ref[...] = jnp.where(lane == 0, i1, jnp.where(lane == 1, i2, jnp.where(lane == 2, p1, jnp.where(lane == 3, p2, 0))))
    gate_ref[...] = jnp.where(lane == 0, g1, jnp.where(lane == 1, g2, 0.0))
    base_ref[...] += jnp.sum(sel, axis=0, keepdims=True)
    cnt_ref[...] = base_ref[...].astype(jnp.int32)


def _row_gather_start(src_hbm, idx_ref, dst_ref, sem, n_rows):
    for r in range(n_rows):
        pltpu.make_async_copy(src_hbm.at[pl.ds(idx_ref[0, 0, r], 1)], dst_ref.at[pl.ds(r, 1)], sem).start(
            priority=r % DMA_PRIORITIES)


def _row_gather_wait(src_hbm, dst_ref, sem, n_rows):
    pltpu.make_async_copy(src_hbm.at[pl.ds(0, n_rows)], dst_ref, sem).wait()


def _moe_dispatch_kernel(pad_start_ref, pad_len_ref, tail_ref, d1_ref, d2_ref, x_hbm, xs_hbm, xbuf, zbuf,
                         in_sem, sem, pad_sem):
    i = pl.program_id(0)
    n = pl.num_programs(0)
    n_buf, tb, _ = xbuf.shape

    def fetch(j, b):
        return pltpu.make_async_copy(x_hbm.at[pl.ds(pl.multiple_of(j * tb, tb), tb)], xbuf.at[b], in_sem.at[b])

    @pl.when(i == 0)
    def _():
        fetch(0, 0).start()
        zbuf[...] = jnp.zeros(zbuf.shape, F32)
        for e in range(pad_start_ref.shape[0]):
            def zero_row(k, carry, e=e):
                pltpu.make_async_copy(zbuf.at[pl.ds(0, 1)], xs_hbm.at[pl.ds(pad_start_ref[e] + k, 1)], pad_sem).start()
                return carry
            lax.fori_loop(0, pad_len_ref[e], zero_row, 0)

        def zero_piece(k, carry):
            row0 = pl.multiple_of(tail_ref[0] + k * tb, tb)
            pltpu.make_async_copy(zbuf, xs_hbm.at[pl.ds(row0, tb)], pad_sem).start()
            return carry
        lax.fori_loop(0, tail_ref[1], zero_piece, 0)
        for e in range(pad_start_ref.shape[0]):
            def wait_row(k, carry):
                pltpu.make_async_copy(zbuf.at[pl.ds(0, 1)], xs_hbm.at[pl.ds(0, 1)], pad_sem).wait()
                return carry
            lax.fori_loop(0, pad_len_ref[e], wait_row, 0)

        def wait_piece(k, carry):
            pltpu.make_async_copy(zbuf, xs_hbm.at[pl.ds(0, tb)], pad_sem).wait()
            return carry
        lax.fori_loop(0, tail_ref[1], wait_piece, 0)

    @pl.when(i + 1 < n)
    def _():
        fetch(i + 1, (i + 1) % n_buf).start()

    buf = i % n_buf
    fetch(i, buf).wait()
    slot = i % 2
    for r in range(tb):
        row = xbuf.at[buf, pl.ds(r, 1)]
        pltpu.make_async_copy(row, xs_hbm.at[pl.ds(d1_ref[0, 0, r], 1)], sem.at[slot]).start(priority=0)
        pltpu.make_async_copy(row, xs_hbm.at[pl.ds(d2_ref[0, 0, r], 1)], sem.at[slot]).start(priority=1)

    def wait_tile(s):
        for _ in range(2):
            pltpu.make_async_copy(xbuf.at[0], xs_hbm.at[pl.ds(0, tb)], sem.at[s]).wait()

    @pl.when(i > 0)
    def _():
        wait_tile(1 - slot)

    @pl.when(i == n - 1)
    def _():
        wait_tile(slot)


def _moe_expert_kernel(tile_e_ref, n_used_ref, xs_ref, w1_ref, w3_ref, w2_ref, y_ref, w1b_ref, w3b_ref, w2b_ref):
    i = pl.program_id(0)
    used = i < n_used_ref[0]

    @pl.when((i == 0) | (tile_e_ref[i] != tile_e_ref[jnp.maximum(i - 1, 0)]))
    def _():
        w1b_ref[...] = w1_ref[0].astype(BF16)
        w3b_ref[...] = w3_ref[0].astype(BF16)
        w2b_ref[...] = w2_ref[0].astype(BF16)

    @pl.when(used)
    def _():
        xb = xs_ref[...].astype(BF16)
        hid = _silu(jnp.dot(xb, w1b_ref[...], preferred_element_type=F32)) * jnp.dot(xb, w3b_ref[...], preferred_element_type=F32)
        y_ref[...] = jnp.dot(hid.astype(BF16), w2b_ref[...], preferred_element_type=F32)

    @pl.when(jnp.logical_not(used))
    def _():
        y_ref[...] = jnp.zeros(y_ref.shape, F32)


def _moe_combine_kernel(d1_ref, d2_ref, d1n_ref, d2n_ref, x_ref, gate_ref, y_hbm, g_ref, b_ref, o_ref, ybuf, sem,
                        *, alpha):
    i = pl.program_id(0)
    n = pl.num_programs(0)
    nf, cb, d = x_ref.shape
    tb = nf * cb
    slot = i % 2

    def start(a_ref, b_ref, s):
        _row_gather_start(y_hbm, a_ref, ybuf.at[s, 0], sem.at[s], tb)
        _row_gather_start(y_hbm, b_ref, ybuf.at[s, 1], sem.at[s], tb)

    @pl.when(i == 0)
    def _():
        start(d1_ref, d2_ref, 0)

    @pl.when(i + 1 < n)
    def _():
        start(d1n_ref, d2n_ref, 1 - slot)

    _row_gather_wait(y_hbm, ybuf.at[slot, 0], sem.at[slot], tb)
    _row_gather_wait(y_hbm, ybuf.at[slot, 1], sem.at[slot], tb)
    gate = gate_ref[...].reshape(tb, gate_ref.shape[-1])
    f = gate[:, 0:1] * ybuf[slot, 0] + gate[:, 1:2] * ybuf[slot, 1]
    y = _layer_norm(alpha * x_ref[...].reshape(tb, d) + f, g_ref[...], b_ref[...])
    for k in range(nf):
        o_ref[:, k, :] = y[k * cb:(k + 1) * cb, :]


def _moe_layer(xt, w_router, b_router, w1, w3, w2, ln_g, ln_b, alpha, tm=1024, tr=256, tb=256):
    n_f, n_c, d = xt.shape
    t = n_f * n_c
    x2 = xt.reshape(t, d)
    n_e, _, dff = w1.shape
    tm = min(tm, t)
    tb = min(tb, COMBINE_FRAMES * n_c)
    assert tr % tb == 0, "the grouped buffer's tail is zeroed in tb-row pieces"
    idx = jnp.arange(tm)
    low = (idx[:, None] > idx[None, :]).astype(BF16)
    wr_hi = w_router.astype(BF16)
    route_i, route_g, counts = pl.pallas_call(
        _moe_route_kernel,
        grid=(t // tm,),
        in_specs=[pl.BlockSpec((tm, d), lambda i: (i, 0)), _const_spec((d, n_e)), _const_spec((d, n_e)),
                  _const_spec((1, n_e)), _const_spec((tm, tm))],
        out_specs=[pl.BlockSpec((tm, n_e), lambda i: (i, 0)), pl.BlockSpec((tm, n_e), lambda i: (i, 0)),
                   pl.BlockSpec((1, n_e), lambda i: (0, 0))],
        out_shape=[jax.ShapeDtypeStruct((t, n_e), jnp.int32), jax.ShapeDtypeStruct((t, n_e), F32),
                   jax.ShapeDtypeStruct((1, n_e), jnp.int32)],
        scratch_shapes=[pltpu.VMEM((1, n_e), F32)],
        compiler_params=_params("arbitrary"),
        name="moe_route",
    )(x2, wr_hi, (w_router.astype(F32) - wr_hi.astype(F32)).astype(BF16), b_router.reshape(1, n_e).astype(F32), low)

    n_tiles = (2 * t) // tr + n_e
    cnt = counts[0]
    size = ((cnt + tr - 1) // tr) * tr
    ends = jnp.cumsum(size)
    offs = ends - size
    e1, e2, p1, p2 = route_i[:, 0], route_i[:, 1], route_i[:, 2], route_i[:, 3]
    dest1 = offs[e1] + p1
    dest2 = offs[e2] + p2
    tile_e = jnp.minimum(jnp.sum(jnp.arange(n_tiles, dtype=jnp.int32)[:, None] * tr >= ends[None, :], axis=1),
                         n_e - 1).astype(jnp.int32)
    n_used = (ends[n_e - 1:] // tr).astype(jnp.int32)

    smem_blk = lambda n: pl.BlockSpec((1, 1, n), lambda i, *_: (i, 0, 0), memory_space=pltpu.SMEM)
    smem_next = lambda n, last: pl.BlockSpec((1, 1, n), lambda i, *_: (jnp.minimum(i + 1, last), 0, 0),
                                             memory_space=pltpu.SMEM)
    any_spec = pl.BlockSpec(memory_space=pl.ANY)
    xs = pl.pallas_call(
        _moe_dispatch_kernel,
        grid_spec=pltpu.PrefetchScalarGridSpec(
            num_scalar_prefetch=3,
            grid=(t // tb,),
            in_specs=[smem_blk(tb), smem_blk(tb), any_spec],
            out_specs=any_spec,
            scratch_shapes=[pltpu.VMEM((DISPATCH_BUFFERS, tb, d), F32), pltpu.VMEM((tb, d), F32),
                            pltpu.SemaphoreType.DMA((DISPATCH_BUFFERS,)), pltpu.SemaphoreType.DMA((2,)),
                            pltpu.SemaphoreType.DMA(())],
        ),
        out_shape=jax.ShapeDtypeStruct((n_tiles * tr, d), F32),
        compiler_params=_params("arbitrary"),
        name="moe_dispatch",
    )((offs + cnt).astype(jnp.int32), (size - cnt).astype(jnp.int32),
      jnp.concatenate([ends[n_e - 1:], (n_tiles * tr - ends[n_e - 1:]) // tb]).astype(jnp.int32),
      dest1.reshape(t // tb, 1, tb), dest2.reshape(t // tb, 1, tb), x2)

    tile = lambda i, te, nu: (i, 0)
    wspec = lambda shape: pl.BlockSpec(shape, lambda i, te, nu: (te[i], 0, 0), pipeline_mode=pl.Buffered(1))
    ys = pl.pallas_call(
        _moe_expert_kernel,
        grid_spec=pltpu.PrefetchScalarGridSpec(
            num_scalar_prefetch=2,
            grid=(n_tiles,),
            in_specs=[pl.BlockSpec((tr, d), tile), wspec((1, d, dff)), wspec((1, d, dff)), wspec((1, dff, d))],
            out_specs=pl.BlockSpec((tr, d), tile),
            scratch_shapes=[pltpu.VMEM((d, dff), BF16), pltpu.VMEM((d, dff), BF16), pltpu.VMEM((dff, d), BF16)],
        ),
        out_shape=jax.ShapeDtypeStruct((n_tiles * tr, d), F32),
        compiler_params=_params("arbitrary"),
        name="moe_experts",
    )(tile_e, n_used, xs, w1.astype(F32), w3.astype(F32), w2.astype(F32))

    nf = COMBINE_FRAMES
    cb = tb // nf
    n_fb, n_cb = n_f // nf, n_c // cb
    n_tb = n_fb * n_cb
    tile_order = lambda a: a.reshape(n_fb, nf, n_cb, cb).transpose(0, 2, 1, 3).reshape(n_tb, 1, tb)
    d1, d2 = tile_order(dest1), tile_order(dest2)
    in_blk = lambda i: (i // n_cb, i % n_cb, 0)
    out = pl.pallas_call(
        functools.partial(_moe_combine_kernel, alpha=alpha),
        grid=(n_tb,),
        in_specs=[smem_blk(tb), smem_blk(tb), smem_next(tb, n_tb - 1), smem_next(tb, n_tb - 1),
                  pl.BlockSpec((nf, cb, d), in_blk), pl.BlockSpec((nf, cb, n_e), in_blk),
                  pl.BlockSpec(memory_space=pl.ANY), _const_spec((1, d)), _const_spec((1, d))],
        out_specs=pl.BlockSpec((cb, nf, d), lambda i: (i % n_cb, i // n_cb, 0)),
        out_shape=jax.ShapeDtypeStruct((n_c, n_f, d), F32),
        scratch_shapes=[pltpu.VMEM((2, 2, tb, d), F32), pltpu.SemaphoreType.DMA((2,))],
        compiler_params=_params("arbitrary"),
        name="moe_combine",
    )(d1, d2, d1, d2, xt, route_g.reshape(n_f, n_c, n_e), ys, ln_g.reshape(1, d), ln_b.reshape(1, d))
    return out.reshape(t, d)


def kernel(x, gdn_w_in, gdn_conv_w, gdn_a_log, gdn_dt_bias, gdn_norm_g, gdn_w_out, ffn_w1, ffn_w3, ffn_w2,
           s5_w_in, s5_lam_re, s5_lam_im, s5_log_dt, s5_b_re, s5_b_im, s5_c_re, s5_c_im, s5_d, s5_w_glu,
           moe_w_router, moe_b_router, moe_w1, moe_w3, moe_w2, ln_g, ln_b):
    bsz, seq, d = x.shape
    depth = ln_g.shape[0]
    alpha = (2 * depth) ** 0.25
    x2 = x.reshape(bsz * seq, d)
    for i in range(depth):
        j = i // 2
        if i % 2 == 0:
            x2 = _gdn_layer(x2, bsz, seq, gdn_w_in[j], gdn_conv_w[j], gdn_a_log[j], gdn_dt_bias[j], gdn_norm_g[j],
                            gdn_w_out[j], ln_g[i, 0], ln_b[i, 0], alpha)
            x2 = _ffn_ln(x2, ffn_w1[j], ffn_w3[j], ffn_w2[j], ln_g[i, 1], ln_b[i, 1], alpha,
                         chunk_major=i + 1 < depth)
        else:
            xt = _s5_layer(x2, seq, s5_w_in[j], s5_lam_re[j], s5_lam_im[j], s5_log_dt[j], s5_b_re[j], s5_b_im[j],
                           s5_c_re[j], s5_c_im[j], s5_d[j], s5_w_glu[j], ln_g[i, 0], ln_b[i, 0], alpha)
            x2 = _moe_layer(xt, moe_w_router[j], moe_b_router[j], moe_w1[j], moe_w3[j], moe_w2[j],
                            ln_g[i, 1], ln_b[i, 1], alpha)
    return x2.reshape(bsz, seq, d)
```

```python
import functools

import jax
import jax.numpy as jnp
from jax import lax
from jax.experimental import pallas as pl
from jax.experimental.pallas import tpu as pltpu

F32 = jnp.float32
BF16 = jnp.bfloat16

CHUNK = 64
GDN_HEADS = 8
GDN_DK = 128
GDN_CONV = 4
S5_GROUP = 16
S5_TBLK = 16
COMBINE_FRAMES = 8
DISPATCH_BUFFERS = 3
DMA_PRIORITIES = 2
LN_EPS = 1e-5
NORM_EPS = 1e-6

VMEM_LIMIT_BYTES = 56 * 1024 * 1024
CONV_CARRY_ROWS = 8
MXU_COLS = 256
PREP_CHUNKS = 4


def _params(*semantics):
    return pltpu.CompilerParams(dimension_semantics=semantics, vmem_limit_bytes=VMEM_LIMIT_BYTES)


def _const_spec(shape):
    nd = len(shape)
    return pl.BlockSpec(shape, lambda *_: (0,) * nd, pipeline_mode=pl.Buffered(1))


def _dot(a, b):
    return jnp.dot(a.astype(BF16), b.astype(BF16), preferred_element_type=F32)


def _dot_nt(a, b):
    return lax.dot_general(a.astype(BF16), b.astype(BF16), (((1,), (1,)), ((), ())),
                           preferred_element_type=F32)


def _dot_f32(a, b):
    return jnp.dot(a, b, preferred_element_type=F32, precision=lax.Precision.HIGHEST)


def _silu(x):
    return x * jax.nn.sigmoid(x)


def _gelu(x):
    return 0.5 * x * (1.0 + lax.erf(x * (2.0 ** -0.5)))


def _softplus(x):
    return jnp.maximum(x, 0.0) + jnp.log1p(jnp.exp(-jnp.abs(x)))


def _layer_norm(y, g, b):
    mu = jnp.mean(y, axis=-1, keepdims=True)
    yc = y - mu
    var = jnp.mean(yc * yc, axis=-1, keepdims=True)
    return yc * lax.rsqrt(var + LN_EPS) * g + b


def _gdn_in_kernel(x_ref, wqkv_ref, wz_ref, wb_ref, wa_ref, waT_ref, conv_ref, alog_ref, dtb_ref,
                   alogc_ref, dtbc_ref, tri_ref, triT_ref,
                   q_ref, k_ref, v_ref, z_ref, beta_ref, gc_ref, gcT_ref, carry_ref,
                   *, tiles_per_seq, tm, n_qk):
    i = pl.program_id(0)

    @pl.when(i % tiles_per_seq == 0)
    def _():
        carry_ref[...] = jnp.zeros(carry_ref.shape, F32)

    xb = x_ref[...].astype(BF16)
    z_ref[...] = jnp.dot(xb, wz_ref[...], preferred_element_type=F32).astype(z_ref.dtype)
    beta_ref[...] = jax.nn.sigmoid(jnp.dot(xb, wb_ref[...], preferred_element_type=F32))

    g_col = -jnp.exp(alog_ref[...]) * _softplus(jnp.dot(xb, wa_ref[...], preferred_element_type=F32) + dtb_ref[...])
    for c in range(tm // CHUNK):
        rows = slice(c * CHUNK, (c + 1) * CHUNK)
        gc_ref[rows, :] = _dot_f32(tri_ref[...], g_col[rows, :])
        a_row = lax.dot_general(waT_ref[...], xb[rows, :], (((1,), (1,)), ((), ())), preferred_element_type=F32)
        g_row = -jnp.exp(alogc_ref[...]) * _softplus(a_row + dtbc_ref[...])
        gcT_ref[c] = _dot_f32(g_row, triT_ref[...])

    for blk in range(carry_ref.shape[1] // MXU_COLS):
        cols = slice(blk * MXU_COLS, (blk + 1) * MXU_COLS)
        p = jnp.dot(xb, wqkv_ref[:, cols], preferred_element_type=F32)
        ext = jnp.concatenate([carry_ref[:, cols], p], axis=0)
        carry_ref[:, cols] = p[tm - CONV_CARRY_ROWS:tm, :]
        acc = conv_ref[GDN_CONV - 1:GDN_CONV, cols] * p
        for j in range(1, GDN_CONV):
            shifted = pltpu.roll(ext, j, axis=0)[CONV_CARRY_ROWS:, :]
            acc = acc + conv_ref[GDN_CONV - 1 - j:GDN_CONV - j, cols] * shifted
        y = _silu(acc)
        for hh in range(MXU_COLS // GDN_DK):
            head = blk * (MXU_COLS // GDN_DK) + hh
            yh = y[:, hh * GDN_DK:(hh + 1) * GDN_DK]
            if head < 2 * n_qk:
                yh = yh * lax.rsqrt(jnp.sum(yh * yh, axis=-1, keepdims=True) + NORM_EPS)
            if head < n_qk:
                q_ref[:, head * GDN_DK:(head + 1) * GDN_DK] = (yh * (GDN_DK ** -0.5)).astype(q_ref.dtype)
            elif head < 2 * n_qk:
                k_ref[:, (head - n_qk) * GDN_DK:(head - n_qk + 1) * GDN_DK] = yh.astype(k_ref.dtype)
            else:
                v_ref[:, (head - 2 * n_qk) * GDN_DK:(head - 2 * n_qk + 1) * GDN_DK] = yh.astype(v_ref.dtype)


def _gdn_in(x2, w_in, conv_w, a_log, dt_bias, seq, tm=256):
    t, d = x2.shape
    tm = min(tm, seq)
    nh = GDN_HEADS
    qk = nh * GDN_DK
    dv = (w_in.shape[1] - 2 * qk - 2 * nh) // 2
    wqkv = w_in[:, :2 * qk + dv].astype(BF16)
    wz = w_in[:, 2 * qk + dv:2 * qk + 2 * dv].astype(BF16)
    wb = w_in[:, 2 * qk + 2 * dv:2 * qk + 2 * dv + nh].astype(BF16)
    wa = w_in[:, 2 * qk + 2 * dv + nh:].astype(BF16)
    idx = jnp.arange(CHUNK)
    tri = (idx[:, None] >= idx[None, :]).astype(F32)
    n_chunks = t // CHUNK
    kern = functools.partial(_gdn_in_kernel, tiles_per_seq=seq // tm, tm=tm, n_qk=nh)
    row = lambda i: (i, 0)
    return pl.pallas_call(
        kern,
        grid=(t // tm,),
        in_specs=[
            pl.BlockSpec((tm, d), row),
            _const_spec(wqkv.shape), _const_spec(wz.shape), _const_spec(wb.shape), _const_spec(wa.shape),
            _const_spec((nh, d)), _const_spec(conv_w.shape),
            _const_spec((1, nh)), _const_spec((1, nh)), _const_spec((nh, 1)), _const_spec((nh, 1)),
            _const_spec((CHUNK, CHUNK)), _const_spec((CHUNK, CHUNK)),
        ],
        out_specs=[
            pl.BlockSpec((tm, qk), row), pl.BlockSpec((tm, qk), row), pl.BlockSpec((tm, dv), row),
            pl.BlockSpec((tm, dv), row), pl.BlockSpec((tm, nh), row), pl.BlockSpec((tm, nh), row),
            pl.BlockSpec((tm // CHUNK, nh, CHUNK), lambda i: (i, 0, 0)),
        ],
        out_shape=[
            jax.ShapeDtypeStruct((t, qk), BF16), jax.ShapeDtypeStruct((t, qk), BF16),
            jax.ShapeDtypeStruct((t, dv), BF16), jax.ShapeDtypeStruct((t, dv), BF16),
            jax.ShapeDtypeStruct((t, nh), F32), jax.ShapeDtypeStruct((t, nh), F32),
            jax.ShapeDtypeStruct((n_chunks, nh, CHUNK), F32),
        ],
        scratch_shapes=[pltpu.VMEM((CONV_CARRY_ROWS, 2 * qk + dv), F32)],
        compiler_params=_params("arbitrary"),
        name="gdn_in",
    )(x2, wqkv, wz, wb, wa, wa.T, conv_w.astype(F32),
      a_log.reshape(1, nh).astype(F32), dt_bias.reshape(1, nh).astype(F32),
      a_log.reshape(nh, 1).astype(F32), dt_bias.reshape(nh, 1).astype(F32), tri, tri.T)


def _gdn_core_kernel(q_ref, k_ref, v_ref, z_ref, beta_ref, gc_ref, gcT_ref, ng_ref, o_ref,
                     s_ref, u_ref, wq_ref, kd_ref, a_ref, *, n_sub):
    bsz, n_pair = s_ref.shape[0], s_ref.shape[1]
    pw = 2 * GDN_DK
    cc = CHUNK

    @pl.when(pl.program_id(0) == 0)
    def _():
        s_ref[...] = jnp.zeros(s_ref.shape, F32)

    r = lax.broadcasted_iota(jnp.int32, (cc, 2 * cc), 0)
    lane_m = lax.broadcasted_iota(jnp.int32, (cc, 2 * cc), 1)
    c = lane_m & (cc - 1)
    causal = r >= c
    strict = r > c
    eye = jnp.where(r == c, 1.0, 0.0)
    first_m = lane_m < cc
    first_w = lax.broadcasted_iota(jnp.int32, (cc, pw), 1) < GDN_DK
    sr = lax.broadcasted_iota(jnp.int32, (pw, pw), 0) < GDN_DK
    sc = lax.broadcasted_iota(jnp.int32, (pw, pw), 1) < GDN_DK
    same_head = sr == sc
    first_row = lax.broadcasted_iota(jnp.int32, (pw, 1), 0) < GDN_DK

    def block_diag(x, first):
        return jnp.concatenate([jnp.where(first, x, 0.0), jnp.where(first, 0.0, x)], axis=0)

    def pair_cols(col8, p, first):
        return jnp.where(first, col8[:, 2 * p:2 * p + 1], col8[:, 2 * p + 1:2 * p + 2])

    def prep(it, carry):
        chains = []
        for j in range(PREP_CHUNKS):
            idx = it * PREP_CHUNKS + j
            b, ci = idx // n_sub, idx % n_sub
            rows = pl.ds(pl.multiple_of(ci * cc, cc), cc)
            beta, gc, gct = beta_ref[b, rows, :], gc_ref[b, rows, :], gcT_ref[b, ci]
            for p in range(n_pair):
                chains.append((b, ci, rows, p, beta, gc, gct))
        ch = range(len(chains))
        lanes = [slice(p * pw, (p + 1) * pw) for (_, _, _, p, _, _, _) in chains]
        q = [q_ref[b, rows, lanes[i]].astype(F32) for i, (b, _, rows, *_) in enumerate(chains)]
        k = [k_ref[b, rows, lanes[i]].astype(F32) for i, (b, _, rows, *_) in enumerate(chains)]
        v = [v_ref[b, rows, lanes[i]].astype(F32) for i, (b, _, rows, *_) in enumerate(chains)]
        g_m = [pair_cols(gc, p, first_m) for (_, _, _, p, _, gc, _) in chains]
        g_w = [pair_cols(gc, p, first_w) for (_, _, _, p, _, gc, _) in chains]
        b_w = [pair_cols(beta, p, first_w) for (_, _, _, p, beta, _, _) in chains]
        decay = [jnp.exp(jnp.where(causal, g_m[i] - chains[i][6][chains[i][3]:chains[i][3] + 1, :], -jnp.inf)) for i in ch]
        eg = [jnp.exp(g_w[i]) for i in ch]
        kb = [k[i] * b_w[i] for i in ch]
        k_bd = [block_diag(k[i], first_w) for i in ch]
        m = [jnp.where(strict, _dot_nt(kb[i], k_bd[i]) * decay[i], 0.0) for i in ch]
        qk = [_dot_nt(q[i], k_bd[i]) for i in ch]
        t_inv = [eye - m[i] for i in ch]
        step = 1
        while 2 * step <= cc // 2:
            m = [_dot(m[i], block_diag(m[i], first_m)) for i in ch]
            t_inv = [t_inv[i] + _dot(t_inv[i], block_diag(m[i], first_m)) for i in ch]
            step *= 2
        rhs = [jnp.concatenate([block_diag(v[i] * b_w[i], first_w), block_diag(kb[i] * eg[i], first_w)], axis=1)
               for i in ch]
        uw = [_dot(t_inv[i], rhs[i]) for i in ch]
        for i, (b, ci, _, p, _, _, _) in enumerate(chains):
            u_ref[b, ci, p] = uw[i][:, :pw]
            wq_ref[b, ci, p, 0:cc, :] = uw[i][:, pw:].astype(BF16)
            wq_ref[b, ci, p, cc:2 * cc, :] = (q[i] * eg[i]).astype(BF16)
            kd_ref[b, ci, p] = (k[i] * jnp.exp(g_w[i][cc - 1:cc, :] - g_w[i])).T.astype(BF16)
            a_ref[b, ci, p] = (qk[i] * decay[i]).astype(BF16)
        return carry

    lax.fori_loop(0, bsz * n_sub // PREP_CHUNKS, prep, 0)

    chains = [(b, p) for b in range(bsz) for p in range(n_pair)]
    ch = range(len(chains))

    def scan(ci, carry):
        rows = pl.ds(pl.multiple_of(ci * cc, cc), cc)
        e_last = [jnp.exp(gc_ref[b, pl.ds(ci * cc + cc - 1, 1), :]) for b in range(bsz)]
        s = [s_ref[b, p] for (b, p) in chains]
        ws = [jnp.dot(wq_ref[b, ci, p], s[i].astype(BF16), preferred_element_type=F32) for i, (b, p) in enumerate(chains)]
        v_new = [u_ref[b, ci, p] - ws[i][:cc] for i, (b, p) in enumerate(chains)]
        o = [ws[i][cc:] + jnp.dot(a_ref[b, ci, p], block_diag(v_new[i], first_w).astype(BF16),
                                  preferred_element_type=F32) for i, (b, p) in enumerate(chains)]
        upd = [_dot(kd_ref[b, ci, p], v_new[i]) for i, (b, p) in enumerate(chains)]
        for i, (b, p) in enumerate(chains):
            e_col = jnp.where(first_row, e_last[b][:, 2 * p:2 * p + 1], e_last[b][:, 2 * p + 1:2 * p + 2])
            s_ref[b, p] = s[i] * e_col + jnp.where(same_head, upd[i], 0.0)
            for hh in range(2):
                cols = slice((2 * p + hh) * GDN_DK, (2 * p + hh + 1) * GDN_DK)
                oh = o[i][:, hh * GDN_DK:(hh + 1) * GDN_DK]
                on = oh * lax.rsqrt(jnp.mean(oh * oh, axis=-1, keepdims=True) + NORM_EPS) * ng_ref[...]
                o_ref[b, rows, cols] = (on * _silu(z_ref[b, rows, cols].astype(F32))).astype(o_ref.dtype)
        return carry

    lax.fori_loop(0, n_sub, scan, 0)


def _gdn_core(q, k, v, z, beta, gc, gct, norm_g, bsz, seq, tc=256):
    t, qk = q.shape
    tc = min(tc, seq)
    nh = GDN_HEADS
    dv = v.shape[1]
    n_sub = tc // CHUNK
    n_pair = nh // 2
    pw = 2 * GDN_DK
    blk3 = lambda w: pl.BlockSpec((bsz, tc, w), lambda i: (0, i, 0))
    seq3 = lambda a: a.reshape(bsz, seq, a.shape[-1])
    o = pl.pallas_call(
        functools.partial(_gdn_core_kernel, n_sub=n_sub),
        grid=(seq // tc,),
        in_specs=[blk3(qk), blk3(qk), blk3(dv), blk3(dv), blk3(nh), blk3(nh),
                  pl.BlockSpec((bsz, n_sub, n_pair, 2 * CHUNK), lambda i: (0, i, 0, 0)),
                  _const_spec((1, dv // nh))],
        out_specs=blk3(dv),
        out_shape=jax.ShapeDtypeStruct((bsz, seq, dv), BF16),
        scratch_shapes=[pltpu.VMEM((bsz, n_pair, pw, pw), F32),
                        pltpu.VMEM((bsz, n_sub, n_pair, CHUNK, pw), F32),
                        pltpu.VMEM((bsz, n_sub, n_pair, 2 * CHUNK, pw), BF16),
                        pltpu.VMEM((bsz, n_sub, n_pair, pw, CHUNK), BF16),
                        pltpu.VMEM((bsz, n_sub, n_pair, CHUNK, 2 * CHUNK), BF16)],
        compiler_params=_params("arbitrary"),
        name="gdn_core",
    )(seq3(q), seq3(k), seq3(v), seq3(z), seq3(beta), seq3(gc),
      gct.reshape(bsz, seq // CHUNK, n_pair, 2 * CHUNK), norm_g.reshape(1, -1).astype(F32))
    return o.reshape(t, dv)


def _proj_ln_kernel(a_ref, w_ref, x_ref, g_ref, b_ref, o_ref, *, alpha):
    h = jnp.dot(a_ref[...].astype(BF16), w_ref[...], preferred_element_type=F32)
    o_ref[...] = _layer_norm(alpha * x_ref[...] + h, g_ref[...], b_ref[...])


def _proj_ln(a, w, x2, ln_g, ln_b, alpha, tm=512):
    t, d = x2.shape
    tm = min(tm, t)
    row = lambda i: (i, 0)
    return pl.pallas_call(
        functools.partial(_proj_ln_kernel, alpha=alpha),
        grid=(t // tm,),
        in_specs=[pl.BlockSpec((tm, a.shape[1]), row), _const_spec(w.shape), pl.BlockSpec((tm, d), row),
                  _const_spec((1, d)), _const_spec((1, d))],
        out_specs=pl.BlockSpec((tm, d), row),
        out_shape=jax.ShapeDtypeStruct((t, d), F32),
        compiler_params=_params("arbitrary"),
        name="proj_ln",
    )(a, w.astype(BF16), x2, ln_g.reshape(1, d), ln_b.reshape(1, d))


def _ffn_ln_kernel(x_ref, w1_ref, w3_ref, w2_ref, g_ref, b_ref, o_ref, *, alpha, chunk_major):
    x = x_ref[...]
    xb = x.astype(BF16)
    hid = _silu(jnp.dot(xb, w1_ref[...], preferred_element_type=F32)) * jnp.dot(xb, w3_ref[...], preferred_element_type=F32)
    f = jnp.dot(hid.astype(BF16), w2_ref[...], preferred_element_type=F32)
    y = _layer_norm(alpha * x + f, g_ref[...], b_ref[...])
    if chunk_major:
        for c in range(o_ref.shape[1]):
            o_ref[:, c, :] = y[c * CHUNK:(c + 1) * CHUNK, :]
    else:
        o_ref[...] = y


def _ffn_ln(x2, w1, w3, w2, ln_g, ln_b, alpha, chunk_major=False, tm=512):
    t, d = x2.shape
    tm = min(tm, t)
    row = lambda i: (i, 0)
    if chunk_major:
        out_spec = pl.BlockSpec((CHUNK, tm // CHUNK, d), lambda i: (0, i, 0))
        out_shape = jax.ShapeDtypeStruct((CHUNK, t // CHUNK, d), F32)
    else:
        out_spec = pl.BlockSpec((tm, d), row)
        out_shape = jax.ShapeDtypeStruct((t, d), F32)
    return pl.pallas_call(
        functools.partial(_ffn_ln_kernel, alpha=alpha, chunk_major=chunk_major),
        grid=(t // tm,),
        in_specs=[pl.BlockSpec((tm, d), row), _const_spec(w1.shape), _const_spec(w3.shape), _const_spec(w2.shape),
                  _const_spec((1, d)), _const_spec((1, d))],
        out_specs=out_spec,
        out_shape=out_shape,
        compiler_params=_params("arbitrary"),
        name="ffn_ln",
    )(x2, w1.astype(BF16), w3.astype(BF16), w2.astype(BF16), ln_g.reshape(1, d), ln_b.reshape(1, d))


def _gdn_layer(x2, bsz, seq, w_in, conv_w, a_log, dt_bias, norm_g, w_out, ln_g, ln_b, alpha):
    q, k, v, z, beta, gc, gct = _gdn_in(x2, w_in, conv_w, a_log, dt_bias, seq)
    o = _gdn_core(q, k, v, z, beta, gc, gct, norm_g, bsz, seq)
    return _proj_ln(o, w_out, x2, ln_g, ln_b, alpha)


def _s5_taps_kernel(ce_ref, b_ref, o_ref):
    o_ref[0] = _dot_f32(ce_ref[0], b_ref[0])


def _s5_tables(lam_re, lam_im, log_dt, b_re, b_im, c_re, c_im, d_skip, n_per_seq):
    g, p = lam_re.shape
    gs = b_re.shape[-1]
    lr, li = lam_re.astype(F32), lam_im.astype(F32)
    dt = jnp.exp(log_dt.astype(F32))[:, None]
    zr, zi = lr * dt, li * dt

    def a_pow(steps):
        s = steps.astype(F32)[:, None, None]
        mag = jnp.exp(zr[None] * s)
        return mag * jnp.cos(zi[None] * s), mag * jnp.sin(zi[None] * s)

    ar, ai = a_pow(jnp.arange(CHUNK + 1))
    nr, ni = ar[1] - 1.0, ai[1]
    den = lr * lr + li * li
    qr, qi = (nr * lr + ni * li) / den, (ni * lr - nr * li) / den
    bbr = qr[..., None] * b_re - qi[..., None] * b_im
    bbi = qr[..., None] * b_im + qi[..., None] * b_re
    cr, ci = c_re.astype(F32), c_im.astype(F32)
    car = cr[:, None] * jnp.moveaxis(ar, 0, 1)[:, :, None, :] - ci[:, None] * jnp.moveaxis(ai, 0, 1)[:, :, None, :]
    cai = cr[:, None] * jnp.moveaxis(ai, 0, 1)[:, :, None, :] + ci[:, None] * jnp.moveaxis(ar, 0, 1)[:, :, None, :]
    ca = jnp.concatenate([car, -cai], axis=-1)
    ce = ca[:, :CHUNK].reshape(g, CHUNK * gs, 2 * p)
    bst = jnp.concatenate([bbr, bbi], axis=1)
    taps = pl.pallas_call(
        _s5_taps_kernel,
        grid=(g,),
        in_specs=[pl.BlockSpec((1, CHUNK * gs, 2 * p), lambda i: (i, 0, 0)),
                  pl.BlockSpec((1, 2 * p, gs), lambda i: (i, 0, 0))],
        out_specs=pl.BlockSpec((1, CHUNK * gs, gs), lambda i: (i, 0, 0)),
        out_shape=jax.ShapeDtypeStruct((g, CHUNK * gs, gs), F32),
        compiler_params=_params("arbitrary"),
        name="s5_taps",
    )(ce, bst).reshape(g, CHUNK, gs, gs)
    taps = taps.at[:, 0].add(d_skip.astype(F32).reshape(g, gs)[:, :, None] * jnp.eye(gs, dtype=F32))
    nb = CHUNK // S5_TBLK
    blk = jnp.arange(S5_TBLK)
    lag = S5_TBLK * jnp.arange(nb)[:, None, None] + blk[None, :, None] - blk[None, None, :]
    stk = jnp.where((lag >= 0)[None, :, :, :, None, None], taps[:, jnp.clip(lag, 0, CHUNK - 1)], 0.0)
    m_mat = jnp.transpose(stk, (0, 1, 2, 4, 3, 5)).reshape(g, CHUNK * gs, S5_TBLK * gs).astype(BF16)
    arr, aii = jnp.moveaxis(ar[:CHUNK][::-1], 0, 1), jnp.moveaxis(ai[:CHUNK][::-1], 0, 1)
    wr = arr[:, :, :, None] * bbr[:, None] - aii[:, :, :, None] * bbi[:, None]
    wi = arr[:, :, :, None] * bbi[:, None] + aii[:, :, :, None] * bbr[:, None]
    w_mat = jnp.concatenate([jnp.transpose(wr, (0, 2, 1, 3)), jnp.transpose(wi, (0, 2, 1, 3))], axis=1)
    w_mat = w_mat.reshape(g, 2 * p, CHUNK * gs).astype(BF16)
    v_mat = ca[:, 1:].reshape(g, CHUNK * gs, 2 * p).astype(BF16)
    n_steps = max(1, (n_per_seq - 1).bit_length())
    apr, api = a_pow(CHUNK * 2 ** jnp.arange(n_steps))
    apr = jnp.moveaxis(apr, 0, 1)[..., None]
    api = jnp.moveaxis(api, 0, 1)[..., None]
    return m_mat, w_mat, v_mat, apr, api


def _s5_in_kernel(x_ref, wT_ref, o_ref):
    o_ref[0] = _dot_nt(wT_ref[...], x_ref[0]).astype(o_ref.dtype)


def _s5_scan_kernel(u_ref, m_ref, w_ref, v_ref, apr_ref, api_ref, o_ref, *, n_per_seq):
    rows, gs, cols = u_ref.shape
    p = apr_ref.shape[2]
    u = u_ref[...].reshape(rows * gs, cols)
    hin = jnp.dot(w_ref[0], u, preferred_element_type=F32)
    yr, yi = hin[:p], hin[p:]
    pos = lax.broadcasted_iota(jnp.int32, (p, cols), 1) % n_per_seq

    def shifted(a, d):
        return jnp.where(pos >= d, pltpu.roll(a, d, axis=1), 0.0)

    d, kk = 1, 0
    while d < n_per_seq:
        sr, si = shifted(yr, d), shifted(yi, d)
        ar, ai = apr_ref[0, kk], api_ref[0, kk]
        yr, yi = yr + ar * sr - ai * si, yi + ar * si + ai * sr
        d, kk = 2 * d, kk + 1
    h_prev = jnp.concatenate([shifted(yr, 1), shifted(yi, 1)], axis=0).astype(BF16)
    nb = rows // S5_TBLK
    br = S5_TBLK * gs
    ys = [jnp.dot(v_ref[0, b * br:(b + 1) * br, :], h_prev, preferred_element_type=F32) for b in range(nb)]
    for sb in range(nb):
        part = jnp.dot(m_ref[0, 0:(nb - sb) * br, :], u[sb * br:(sb + 1) * br, :], preferred_element_type=F32)
        for j in range(nb - sb):
            ys[sb + j] = ys[sb + j] + part[j * br:(j + 1) * br, :]
    y = jnp.concatenate(ys, axis=0)
    o_ref[0] = _gelu(y).reshape(rows, gs, cols).astype(o_ref.dtype)


def _s5_out_kernel(h_ref, w_ref, x_ref, g_ref, b_ref, o_ref, *, alpha):
    d = x_ref.shape[-1]
    ht = h_ref[...].reshape(d, h_ref.shape[-1])
    vg = lax.dot_general(ht, w_ref[...], (((0,), (0,)), ((), ())), preferred_element_type=F32)
    o_ref[0] = _layer_norm(alpha * x_ref[0] + vg[:, :d] * jax.nn.sigmoid(vg[:, d:]), g_ref[...], b_ref[...])


def _s5_layer(xt, seq, w_in, lam_re, lam_im, log_dt, b_re, b_im, c_re, c_im, d_skip, w_glu, ln_g, ln_b, alpha):
    _, cols, d = xt.shape
    n_per_seq = seq // CHUNK
    g, p = lam_re.shape
    gs = d // g
    m_mat, w_mat, v_mat, apr, api = _s5_tables(lam_re, lam_im, log_dt, b_re, b_im, c_re, c_im, d_skip, n_per_seq)
    u_t = pl.pallas_call(
        _s5_in_kernel,
        grid=(CHUNK,),
        in_specs=[pl.BlockSpec((1, cols, d), lambda i: (i, 0, 0)), _const_spec((d, d))],
        out_specs=pl.BlockSpec((1, d, cols), lambda i: (i, 0, 0)),
        out_shape=jax.ShapeDtypeStruct((CHUNK, d, cols), BF16),
        compiler_params=_params("arbitrary"),
        name="s5_in",
    )(xt, w_in.T.astype(BF16))
    n_steps = apr.shape[1]
    hid = pl.pallas_call(
        functools.partial(_s5_scan_kernel, n_per_seq=n_per_seq),
        grid=(g,),
        in_specs=[pl.BlockSpec((CHUNK, gs, cols), lambda i: (0, i, 0)),
                  pl.BlockSpec((1, CHUNK * gs, S5_TBLK * gs), lambda i: (i, 0, 0)),
                  pl.BlockSpec((1, 2 * p, CHUNK * gs), lambda i: (i, 0, 0)),
                  pl.BlockSpec((1, CHUNK * gs, 2 * p), lambda i: (i, 0, 0)),
                  pl.BlockSpec((1, n_steps, p, 1), lambda i: (i, 0, 0, 0)),
                  pl.BlockSpec((1, n_steps, p, 1), lambda i: (i, 0, 0, 0))],
        out_specs=pl.BlockSpec((1, CHUNK, gs, cols), lambda i: (i, 0, 0, 0)),
        out_shape=jax.ShapeDtypeStruct((g, CHUNK, gs, cols), BF16),
        compiler_params=_params("arbitrary"),
        name="s5_scan",
    )(u_t, m_mat, w_mat, v_mat, apr, api)
    return pl.pallas_call(
        functools.partial(_s5_out_kernel, alpha=alpha),
        grid=(CHUNK,),
        in_specs=[pl.BlockSpec((g, 1, gs, cols), lambda i: (0, i, 0, 0)), _const_spec(w_glu.shape),
                  pl.BlockSpec((1, cols, d), lambda i: (i, 0, 0)), _const_spec((1, d)), _const_spec((1, d))],
        out_specs=pl.BlockSpec((1, cols, d), lambda i: (i, 0, 0)),
        out_shape=jax.ShapeDtypeStruct((CHUNK, cols, d), F32),
        compiler_params=_params("arbitrary"),
        name="s5_out",
    )(hid, w_glu.astype(BF16), xt, ln_g.reshape(1, d), ln_b.reshape(1, d))


def _top2_gates(logits):
    n_e = logits.shape[-1]
    lane = lax.broadcasted_iota(jnp.int32, logits.shape, 1)
    m1 = jnp.max(logits, axis=-1, keepdims=True)
    i1 = jnp.min(jnp.where(logits == m1, lane, n_e), axis=-1, keepdims=True)
    rest = jnp.where(lane == i1, -jnp.inf, logits)
    m2 = jnp.max(rest, axis=-1, keepdims=True)
    i2 = jnp.min(jnp.where(rest == m2, lane, n_e), axis=-1, keepdims=True)
    e2 = jnp.exp(m2 - m1)
    den = 1.0 + e2
    return i1, i2, 1.0 / den, e2 / den


def _moe_route_kernel(x_ref, wh_ref, wl_ref, br_ref, low_ref, idx_ref, gate_ref, cnt_ref, base_ref):
    @pl.when(pl.program_id(0) == 0)
    def _():
        base_ref[...] = jnp.zeros(base_ref.shape, F32)

    x = x_ref[...]
    xh = x.astype(BF16)
    xl = (x - xh.astype(F32)).astype(BF16)
    logits = (jnp.dot(xh, wh_ref[...], preferred_element_type=F32)
              + (jnp.dot(xh, wl_ref[...], preferred_element_type=F32) + jnp.dot(xl, wh_ref[...], preferred_element_type=F32)))
    i1, i2, g1, g2 = _top2_gates(logits + br_ref[...])
    lane = lax.broadcasted_iota(jnp.int32, (x_ref.shape[0], wh_ref.shape[1]), 1)
    sel = jnp.where((lane == i1) | (lane == i2), 1.0, 0.0)
    rank = base_ref[...] + jnp.dot(low_ref[...], sel.astype(BF16), preferred_element_type=F32)
    p1 = jnp.sum(jnp.where(lane == i1, rank, 0.0), axis=-1, keepdims=True).astype(jnp.int32)
    p2 = jnp.sum(jnp.where(lane == i2, rank, 0.0), axis=-1, keepdims=True).astype(jnp.int32)
    idx_ref[...] = jnp.where(lane == 0, i1, jnp.where(lane == 1, i2, jnp.where(lane == 2, p1, jnp.where(lane == 3, p2, 0))))
    gate_ref[...] = jnp.where(lane == 0, g1, jnp.where(lane == 1, g2, 0.0))
    base_ref[...] += jnp.sum(sel, axis=0, keepdims=True)
    cnt_ref[...] = base_ref[...].astype(jnp.int32)


def _row_gather_start(src_hbm, idx_ref, dst_ref, sem, n_rows):
    for r in range(n_rows):
        pltpu.make_async_copy(src_hbm.at[pl.ds(idx_ref[0, 0, r], 1)], dst_ref.at[pl.ds(r, 1)], sem).start(
            priority=r % DMA_PRIORITIES)


def _row_gather_wait(src_hbm, dst_ref, sem, n_rows):
    pltpu.make_async_copy(src_hbm.at[pl.ds(0, n_rows)], dst_ref, sem).wait()


def _moe_dispatch_kernel(pad_start_ref, pad_len_ref, tail_ref, d1_ref, d2_ref, x_hbm, xs_hbm, xbuf, zbuf,
                         in_sem, sem, pad_sem):
    i = pl.program_id(0)
    n = pl.num_programs(0)
    n_buf, tb, _ = xbuf.shape

    def fetch(j, b):
        return pltpu.make_async_copy(x_hbm.at[pl.ds(pl.multiple_of(j * tb, tb), tb)], xbuf.at[b], in_sem.at[b])

    @pl.when(i == 0)
    def _():
        fetch(0, 0).start()
        zbuf[...] = jnp.zeros(zbuf.shape, F32)
        for e in range(pad_start_ref.shape[0]):
            def zero_row(k, carry, e=e):
                pltpu.make_async_copy(zbuf.at[pl.ds(0, 1)], xs_hbm.at[pl.ds(pad_start_ref[e] + k, 1)], pad_sem).start()
                return carry
            lax.fori_loop(0, pad_len_ref[e], zero_row, 0)

        def zero_piece(k, carry):
            row0 = pl.multiple_of(tail_ref[0] + k * tb, tb)
            pltpu.make_async_copy(zbuf, xs_hbm.at[pl.ds(row0, tb)], pad_sem).start()
            return carry
        lax.fori_loop(0, tail_ref[1], zero_piece, 0)
        for e in range(pad_start_ref.shape[0]):
            def wait_row(k, carry):
                pltpu.make_async_copy(zbuf.at[pl.ds(0, 1)], xs_hbm.at[pl.ds(0, 1)], pad_sem).wait()
                return carry
            lax.fori_loop(0, pad_len_ref[e], wait_row, 0)

        def wait_piece(k, carry):
            pltpu.make_async_copy(zbuf, xs_hbm.at[pl.ds(0, tb)], pad_sem).wait()
            return carry
        lax.fori_loop(0, tail_ref[1], wait_piece, 0)

    @pl.when(i + 1 < n)
    def _():
        fetch(i + 1, (i + 1) % n_buf).start()

    buf = i % n_buf
    fetch(i, buf).wait()
    slot = i % 2
    for r in range(tb):
        row = xbuf.at[buf, pl.ds(r, 1)]
        pltpu.make_async_copy(row, xs_hbm.at[pl.ds(d1_ref[0, 0, r], 1)], sem.at[slot]).start(priority=0)
        pltpu.make_async_copy(row, xs_hbm.at[pl.ds(d2_ref[0, 0, r], 1)], sem.at[slot]).start(priority=1)

    def wait_tile(s):
        for _ in range(2):
            pltpu.make_async_copy(xbuf.at[0], xs_hbm.at[pl.ds(0, tb)], sem.at[s]).wait()

    @pl.when(i > 0)
    def _():
        wait_tile(1 - slot)

    @pl.when(i == n - 1)
    def _():
        wait_tile(slot)


def _moe_expert_kernel(tile_e_ref, n_used_ref, xs_ref, w1_ref, w3_ref, w2_ref, y_ref, w1b_ref, w3b_ref, w2b_ref):
    i = pl.program_id(0)
    used = i < n_used_ref[0]

    @pl.when((i == 0) | (tile_e_ref[i] != tile_e_ref[jnp.maximum(i - 1, 0)]))
    def _():
        w1b_ref[...] = w1_ref[0].astype(BF16)
        w3b_ref[...] = w3_ref[0].astype(BF16)
        w2b_ref[...] = w2_ref[0].astype(BF16)

    @pl.when(used)
    def _():
        xb = xs_ref[...].astype(BF16)
        hid = _silu(jnp.dot(xb, w1b_ref[...], preferred_element_type=F32)) * jnp.dot(xb, w3b_ref[...], preferred_element_type=F32)
        y_ref[...] = jnp.dot(hid.astype(BF16), w2b_ref[...], preferred_element_type=F32)

    @pl.when(jnp.logical_not(used))
    def _():
        y_ref[...] = jnp.zeros(y_ref.shape, F32)


def _moe_combine_kernel(d1_ref, d2_ref, d1n_ref, d2n_ref, x_ref, gate_ref, y_hbm, g_ref, b_ref, o_ref, ybuf, sem,
                        *, alpha):
    i = pl.program_id(0)
    n = pl.num_programs(0)
    nf, cb, d = x_ref.shape
    tb = nf * cb
    slot = i % 2

    def start(a_ref, b_ref, s):
        _row_gather_start(y_hbm, a_ref, ybuf.at[s, 0], sem.at[s], tb)
        _row_gather_start(y_hbm, b_ref, ybuf.at[s, 1], sem.at[s], tb)

    @pl.when(i == 0)
    def _():
        start(d1_ref, d2_ref, 0)

    @pl.when(i + 1 < n)
    def _():
        start(d1n_ref, d2n_ref, 1 - slot)

    _row_gather_wait(y_hbm, ybuf.at[slot, 0], sem.at[slot], tb)
    _row_gather_wait(y_hbm, ybuf.at[slot, 1], sem.at[slot], tb)
    gate = gate_ref[...].reshape(tb, gate_ref.shape[-1])
    f = gate[:, 0:1] * ybuf[slot, 0] + gate[:, 1:2] * ybuf[slot, 1]
    y = _layer_norm(alpha * x_ref[...].reshape(tb, d) + f, g_ref[...], b_ref[...])
    for k in range(nf):
        o_ref[:, k, :] = y[k * cb:(k + 1) * cb, :]


def _moe_layer(xt, w_router, b_router, w1, w3, w2, ln_g, ln_b, alpha, tm=1024, tr=256, tb=256):
    n_f, n_c, d = xt.shape
    t = n_f * n_c
    x2 = xt.reshape(t, d)
    n_e, _, dff = w1.shape
    tm = min(tm, t)
    tb = min(tb, COMBINE_FRAMES * n_c)
    assert tr % tb == 0, "the grouped buffer's tail is zeroed in tb-row pieces"
    idx = jnp.arange(tm)
    low = (idx[:, None] > idx[None, :]).astype(BF16)
    wr_hi = w_router.astype(BF16)
    route_i, route_g, counts = pl.pallas_call(
        _moe_route_kernel,
        grid=(t // tm,),
        in_specs=[pl.BlockSpec((tm, d), lambda i: (i, 0)), _const_spec((d, n_e)), _const_spec((d, n_e)),
                  _const_spec((1, n_e)), _const_spec((tm, tm))],
        out_specs=[pl.BlockSpec((tm, n_e), lambda i: (i, 0)), pl.BlockSpec((tm, n_e), lambda i: (i, 0)),
                   pl.BlockSpec((1, n_e), lambda i: (0, 0))],
        out_shape=[jax.ShapeDtypeStruct((t, n_e), jnp.int32), jax.ShapeDtypeStruct((t, n_e), F32),
                   jax.ShapeDtypeStruct((1, n_e), jnp.int32)],
        scratch_shapes=[pltpu.VMEM((1, n_e), F32)],
        compiler_params=_params("arbitrary"),
        name="moe_route",
    )(x2, wr_hi, (w_router.astype(F32) - wr_hi.astype(F32)).astype(BF16), b_router.reshape(1, n_e).astype(F32), low)

    n_tiles = (2 * t) // tr + n_e
    cnt = counts[0]
    size = ((cnt + tr - 1) // tr) * tr
    ends = jnp.cumsum(size)
    offs = ends - size
    e1, e2, p1, p2 = route_i[:, 0], route_i[:, 1], route_i[:, 2], route_i[:, 3]
    dest1 = offs[e1] + p1
    dest2 = offs[e2] + p2
    tile_e = jnp.minimum(jnp.sum(jnp.arange(n_tiles, dtype=jnp.int32)[:, None] * tr >= ends[None, :], axis=1),
                         n_e - 1).astype(jnp.int32)
    n_used = (ends[n_e - 1:] // tr).astype(jnp.int32)

    smem_blk = lambda n: pl.BlockSpec((1, 1, n), lambda i, *_: (i, 0, 0), memory_space=pltpu.SMEM)
    smem_next = lambda n, last: pl.BlockSpec((1, 1, n), lambda i, *_: (jnp.minimum(i + 1, last), 0, 0),
                                             memory_space=pltpu.SMEM)
    any_spec = pl.BlockSpec(memory_space=pl.ANY)
    xs = pl.pallas_call(
        _moe_dispatch_kernel,
        grid_spec=pltpu.PrefetchScalarGridSpec(
            num_scalar_prefetch=3,
            grid=(t // tb,),
            in_specs=[smem_blk(tb), smem_blk(tb), any_spec],
            out_specs=any_spec,
            scratch_shapes=[pltpu.VMEM((DISPATCH_BUFFERS, tb, d), F32), pltpu.VMEM((tb, d), F32),
                            pltpu.SemaphoreType.DMA((DISPATCH_BUFFERS,)), pltpu.SemaphoreType.DMA((2,)),
                            pltpu.SemaphoreType.DMA(())],
        ),
        out_shape=jax.ShapeDtypeStruct((n_tiles * tr, d), F32),
        compiler_params=_params("arbitrary"),
        name="moe_dispatch",
    )((offs + cnt).astype(jnp.int32), (size - cnt).astype(jnp.int32),
      jnp.concatenate([ends[n_e - 1:], (n_tiles * tr - ends[n_e - 1:]) // tb]).astype(jnp.int32),
      dest1.reshape(t // tb, 1, tb), dest2.reshape(t // tb, 1, tb), x2)

    tile = lambda i, te, nu: (i, 0)
    wspec = lambda shape: pl.BlockSpec(shape, lambda i, te, nu: (te[i], 0, 0))
    ys = pl.pallas_call(
        _moe_expert_kernel,
        grid_spec=pltpu.PrefetchScalarGridSpec(
            num_scalar_prefetch=2,
            grid=(n_tiles,),
            in_specs=[pl.BlockSpec((tr, d), tile), wspec((1, d, dff)), wspec((1, d, dff)), wspec((1, dff, d))],
            out_specs=pl.BlockSpec((tr, d), tile),
            scratch_shapes=[pltpu.VMEM((d, dff), BF16), pltpu.VMEM((d, dff), BF16), pltpu.VMEM((dff, d), BF16)],
        ),
        out_shape=jax.ShapeDtypeStruct((n_tiles * tr, d), F32),
        compiler_params=_params("arbitrary"),
        name="moe_experts",
    )(tile_e, n_used, xs, w1.astype(F32), w3.astype(F32), w2.astype(F32))

    nf = COMBINE_FRAMES
    cb = tb // nf
    n_fb, n_cb = n_f // nf, n_c // cb
    n_tb = n_fb * n_cb
    tile_order = lambda a: a.reshape(n_fb, nf, n_cb, cb).transpose(0, 2, 1, 3).reshape(n_tb, 1, tb)
    d1, d2 = tile_order(dest1), tile_order(dest2)
    in_blk = lambda i: (i // n_cb, i % n_cb, 0)
    out = pl.pallas_call(
        functools.partial(_moe_combine_kernel, alpha=alpha),
        grid=(n_tb,),
        in_specs=[smem_blk(tb), smem_blk(tb), smem_next(tb, n_tb - 1), smem_next(tb, n_tb - 1),
                  pl.BlockSpec((nf, cb, d), in_blk), pl.BlockSpec((nf, cb, n_e), in_blk),
                  pl.BlockSpec(memory_space=pl.ANY), _const_spec((1, d)), _const_spec((1, d))],
        out_specs=pl.BlockSpec((cb, nf, d), lambda i: (i % n_cb, i // n_cb, 0)),
        out_shape=jax.ShapeDtypeStruct((n_c, n_f, d), F32),
        scratch_shapes=[pltpu.VMEM((2, 2, tb, d), F32), pltpu.SemaphoreType.DMA((2,))],
        compiler_params=_params("arbitrary"),
        name="moe_combine",
    )(d1, d2, d1, d2, xt, route_g.reshape(n_f, n_c, n_e), ys, ln_g.reshape(1, d), ln_b.reshape(1, d))
    return out.reshape(t, d)


def kernel(x, gdn_w_in, gdn_conv_w, gdn_a_log, gdn_dt_bias, gdn_norm_g, gdn_w_out, ffn_w1, ffn_w3, ffn_w2,
           s5_w_in, s5_lam_re, s5_lam_im, s5_log_dt, s5_b_re, s5_b_im, s5_c_re, s5_c_im, s5_d, s5_w_glu,
           moe_w_router, moe_b_router, moe_w1, moe_w3, moe_w2, ln_g, ln_b):
    bsz, seq, d = x.shape
    depth = ln_g.shape[0]
    alpha = (2 * depth) ** 0.25
    x2 = x.reshape(bsz * seq, d)
    for i in range(depth):
        j = i // 2
        if i % 2 == 0:
            x2 = _gdn_layer(x2, bsz, seq, gdn_w_in[j], gdn_conv_w[j], gdn_a_log[j], gdn_dt_bias[j], gdn_norm_g[j],
                            gdn_w_out[j], ln_g[i, 0], ln_b[i, 0], alpha)
            x2 = _ffn_ln(x2, ffn_w1[j], ffn_w3[j], ffn_w2[j], ln_g[i, 1], ln_b[i, 1], alpha,
                         chunk_major=i + 1 < depth)
        else:
            xt = _s5_layer(x2, seq, s5_w_in[j], s5_lam_re[j], s5_lam_im[j], s5_log_dt[j], s5_b_re[j], s5_b_im[j],
                           s5_c_re[j], s5_c_im[j], s5_d[j], s5_w_glu[j], ln_g[i, 0], ln_b[i, 0], alpha)
            x2 = _moe_layer(xt, moe_w_router[j], moe_b_router[j], moe_w1[j], moe_w3[j], moe_w2[j],
                            ln_g[i, 1], ln_b[i, 1], alpha)
    return x2.reshape(bsz, seq, d)
```
